```python
import math
import jax, jax.numpy as jnp
from jax import lax
import numpy as np

D_MODEL = 2048
BATCH = 4
SEQ = 8192
DEPTH = 1

CHUNK = 64
Q_BLOCK = 128
CONV_WIDTH = D_MODEL // 2
CONV_KERNEL = 31
HEAD_DIM = 128
V_HEAD_DIM = 2 * HEAD_DIM
N_HEADS = D_MODEL // (4 * HEAD_DIM)
ATTN_QK_WIDTH = N_HEADS * 2 * HEAD_DIM
ATTN_V_WIDTH = N_HEADS * V_HEAD_DIM
N_BRANCHES = 2
D_FF = 4 * D_MODEL
ROPE_THETA = 10000.0
EPS = 1e-6
IN_WIDTH = 2 * CONV_WIDTH + 2 * ATTN_QK_WIDTH + ATTN_V_WIDTH + N_BRANCHES * D_MODEL

kernel_name = "hybrid_conformer_diffattn_gated_block"


def rmsnorm(t, g):
    tf = t.astype(jnp.float32)
    y = tf * lax.rsqrt(jnp.mean(tf * tf, axis=-1, keepdims=True) + EPS)
    return (y * g.astype(jnp.float32)).astype(t.dtype)


def rope(t, cos, sin):
    half = t.shape[-1] // 2
    tf = t.astype(jnp.float32)
    t1, t2 = tf[..., :half], tf[..., half:]
    return jnp.concatenate([t1 * cos - t2 * sin, t2 * cos + t1 * sin], axis=-1).astype(t.dtype)


def lambda_init_fn(layer_idx):
    return 0.8 - 0.6 * math.exp(-0.3 * layer_idx)


def conformer_conv_branch(conv_a, conv_b_in, conv_w, conv_b, conv_norm_g, w_conv_out):
    glu = conv_a * jax.nn.sigmoid(conv_b_in)
    y = lax.conv_general_dilated(
        glu, conv_w[:, None, :].astype(glu.dtype), window_strides=(1,),
        padding=[(CONV_KERNEL - 1, 0)],
        dimension_numbers=("NWC", "WIO", "NWC"),
        feature_group_count=CONV_WIDTH) + conv_b
    y = jax.nn.silu(rmsnorm(y, conv_norm_g))
    return y @ w_conv_out


def diff_attention_branch(q, k, v, cos, sin, q_norm_g, k_norm_g,
                          lambda_q1, lambda_k1, lambda_q2, lambda_k2, subln_g, w_attn_out, lam_init):
    B, S, _ = q.shape
    nb = S // Q_BLOCK
    q = q.reshape(B, S, N_HEADS, 2, HEAD_DIM)
    k = k.reshape(B, S, N_HEADS, 2, HEAD_DIM)
    q = rope(rmsnorm(q, q_norm_g), cos, sin)
    k = rope(rmsnorm(k, k_norm_g), cos, sin)
    q = jnp.transpose(q, (0, 2, 3, 1, 4))
    k = jnp.transpose(k, (0, 2, 3, 1, 4))
    v = jnp.transpose(v.reshape(B, S, N_HEADS, V_HEAD_DIM), (0, 2, 1, 3))
    lam = (jnp.exp(jnp.sum(lambda_q1.astype(jnp.float32) * lambda_k1.astype(jnp.float32)))
           - jnp.exp(jnp.sum(lambda_q2.astype(jnp.float32) * lambda_k2.astype(jnp.float32)))
           + lam_init)
    scale = 1.0 / math.sqrt(HEAD_DIM)
    key_chunk = jnp.arange(S) // CHUNK
    q_blocks = jnp.moveaxis(q.reshape(B, N_HEADS, 2, nb, Q_BLOCK, HEAD_DIM), 3, 0)

    def attend(args):
        q_blk, blk = args
        s = jnp.einsum("bhmqd,bhmkd->bhmqk", q_blk, k).astype(jnp.float32) * scale
        q_chunk = (blk * Q_BLOCK + jnp.arange(Q_BLOCK)) // CHUNK
        mask = key_chunk[None, :] <= q_chunk[:, None]
        s = jnp.where(mask, s, jnp.finfo(jnp.float32).min)
        p = jax.nn.softmax(s, axis=-1)
        p_diff = p[:, :, 0] - lam * p[:, :, 1]
        return jnp.einsum("bhqk,bhkv->bhqv", p_diff.astype(v.dtype), v)

    out = lax.map(attend, (q_blocks, jnp.arange(nb)))
    out = jnp.moveaxis(out, 0, 2).reshape(B, N_HEADS, S, V_HEAD_DIM)
    out = rmsnorm(out, subln_g) * (1.0 - lam_init)
    out = jnp.transpose(out, (0, 2, 1, 3)).reshape(B, S, ATTN_V_WIDTH)
    return out @ w_attn_out


def setup_inputs(seed: int = 0) -> dict:
    key = jax.random.key(seed)
    ks = jax.random.split(key, 24)
    f32 = jnp.float32

    def nrm(k, shape, s):
        return jax.random.normal(k, shape, f32) * s

    def gain(k, shape):
        return 1.0 + 0.05 * jax.random.normal(k, shape, f32)

    L, D = DEPTH, D_MODEL
    pos_offset = jax.random.randint(ks[2], (BATCH, 1), 0, 1000, dtype=jnp.int32) * CHUNK
    return {
        "x": nrm(ks[0], (BATCH, SEQ, D), 1.0),
        "c": nrm(ks[1], (BATCH, D), 1.0),
        "pos": (pos_offset + jnp.arange(SEQ, dtype=jnp.int32)[None, :]).astype(jnp.int32),
        "ada_w": nrm(ks[3], (L, D, 6 * D), 0.5 * D ** -0.5),
        "ada_b": nrm(ks[4], (L, 6 * D), 0.01),
        "norm_mix_g": gain(ks[5], (L, D)),
        "w_in": nrm(ks[6], (L, D, IN_WIDTH), D ** -0.5),
        "conv_w": nrm(ks[7], (L, CONV_KERNEL, CONV_WIDTH), CONV_KERNEL ** -0.5),
        "conv_b": nrm(ks[8], (L, CONV_WIDTH), 0.01),
        "conv_norm_g": gain(ks[9], (L, CONV_WIDTH)),
        "w_conv_out": nrm(ks[10], (L, CONV_WIDTH, D), CONV_WIDTH ** -0.5),
        "q_norm_g": gain(ks[11], (L, HEAD_DIM)),
        "k_norm_g": gain(ks[12], (L, HEAD_DIM)),
        "lambda_q1": nrm(ks[13], (L, HEAD_DIM), 0.1),
        "lambda_k1": nrm(ks[14], (L, HEAD_DIM), 0.1),
        "lambda_q2": nrm(ks[15], (L, HEAD_DIM), 0.1),
        "lambda_k2": nrm(ks[16], (L, HEAD_DIM), 0.1),
        "subln_g": gain(ks[17], (L, V_HEAD_DIM)),
        "w_attn_out": nrm(ks[18], (L, ATTN_V_WIDTH, D), ATTN_V_WIDTH ** -0.5),
        "gate_b": nrm(ks[19], (L, N_BRANCHES * D), 0.01),
        "w_out": nrm(ks[20], (L, D, D), D ** -0.5),
        "norm_mlp_g": gain(ks[21], (L, D)),
        "w_mlp_in": nrm(ks[22], (L, D, D_FF), D ** -0.5),
        "w_mlp_out": nrm(ks[23], (L, D_FF, D), D_FF ** -0.5),
    }


def reference(x, c, pos, ada_w, ada_b, norm_mix_g, w_in, conv_w, conv_b, conv_norm_g, w_conv_out,
              q_norm_g, k_norm_g, lambda_q1, lambda_k1, lambda_q2, lambda_k2, subln_g, w_attn_out,
              gate_b, w_out, norm_mlp_g, w_mlp_in, w_mlp_out):
    B, S, D = x.shape
    inv_freq = ROPE_THETA ** (-jnp.arange(0, HEAD_DIM, 2, dtype=jnp.float32) / HEAD_DIM)
    ang = pos.astype(jnp.float32)[:, :, None] * inv_freq[None, None, :]
    cos = jnp.cos(ang)[:, :, None, None, :]
    sin = jnp.sin(ang)[:, :, None, None, :]
    c_act = jax.nn.silu(c)
    split_idx = [CONV_WIDTH, 2 * CONV_WIDTH, 2 * CONV_WIDTH + ATTN_QK_WIDTH,
                 2 * CONV_WIDTH + 2 * ATTN_QK_WIDTH, 2 * CONV_WIDTH + 2 * ATTN_QK_WIDTH + ATTN_V_WIDTH]

    for l in range(DEPTH):
        lam_init = lambda_init_fn(l)
        ada = (c_act @ ada_w[l] + ada_b[l])[:, None, :]
        shift_m, scale_m, gate_m, shift_f, scale_f, gate_f = jnp.split(ada, 6, axis=-1)

        h = rmsnorm(x, norm_mix_g[l]) * (1.0 + scale_m) + shift_m
        u = h @ w_in[l]
        conv_a, conv_g, q, k, v, gate_logits = jnp.split(u, split_idx, axis=-1)
        y_conv = conformer_conv_branch(conv_a, conv_g, conv_w[l], conv_b[l], conv_norm_g[l], w_conv_out[l])
        y_attn = diff_attention_branch(q, k, v, cos, sin, q_norm_g[l], k_norm_g[l],
                                       lambda_q1[l], lambda_k1[l], lambda_q2[l], lambda_k2[l],
                                       subln_g[l], w_attn_out[l], lam_init)
        gates = jax.nn.sigmoid(gate_logits + gate_b[l]).reshape(B, S, N_BRANCHES, D)
        merged = gates[:, :, 0] * y_conv + gates[:, :, 1] * y_attn
        x = x + gate_m * (merged @ w_out[l])

        h = rmsnorm(x, norm_mlp_g[l]) * (1.0 + scale_f) + shift_f
        x = x + gate_f * (jnp.square(jax.nn.relu(h @ w_mlp_in[l])) @ w_mlp_out[l])
    return x
```

```python
import functools
import math

import jax
import jax.numpy as jnp
from jax import lax
from jax.experimental import pallas as pl
from jax.experimental.pallas import tpu as pltpu

F32 = jnp.float32
BF16 = jnp.bfloat16

CHUNK = 64
CONV_KERNEL = 31
HEAD_DIM = 128
V_HEAD_DIM = 2 * HEAD_DIM
ROPE_THETA = 10000.0
EPS = 1e-6
LOG2E = math.log2(math.e)

V7X_VMEM_LIMIT_BYTES = 56 * 1024 * 1024
CONV_HALO_ROWS = 32


def _tiles(seq):
    return dict(
        rope=min(1024, seq),
        proj=min(512, seq),
        conv=min(256, seq),
        attn=min(512, seq),
        merge=min(256, seq),
        mlp=min(512, seq),
        mlp_ff=1024,
    )


def _params(n_axes):
    return pltpu.CompilerParams(
        dimension_semantics=("arbitrary",) * n_axes,
        vmem_limit_bytes=V7X_VMEM_LIMIT_BYTES,
    )


def _sigmoid(v):
    return 1.0 / (1.0 + jnp.exp(-v))


def _rms(v, axis_size):
    return lax.rsqrt(jnp.sum(v * v, axis=-1, keepdims=True) * (1.0 / axis_size) + EPS)


def _mod_norm(x, g, ada, shift_row, scale_row):
    y = x * _rms(x, x.shape[-1]) * g
    return y * (1.0 + ada[scale_row:scale_row + 1, :]) + ada[shift_row:shift_row + 1, :]


def _ada_kernel(c_ref, w_ref, b_ref, o_ref):
    c = c_ref[...]
    c_act = c * _sigmoid(c)
    o_ref[...] = jnp.dot(c_act.astype(BF16), w_ref[...].astype(BF16),
                         preferred_element_type=F32) + b_ref[...]


def _ada(c, ada_w, ada_b):
    b, d = c.shape
    n_out = ada_w.shape[1]
    rows = 8
    c_pad = jnp.zeros((rows, d), F32).at[:b].set(c)
    tn = 1536 if n_out % 1536 == 0 else n_out
    out = pl.pallas_call(
        _ada_kernel,
        out_shape=jax.ShapeDtypeStruct((rows, n_out), F32),
        grid=(n_out // tn,),
        in_specs=[
            pl.BlockSpec((rows, d), lambda j: (0, 0)),
            pl.BlockSpec((d, tn), lambda j: (0, j)),
            pl.BlockSpec((1, tn), lambda j: (0, j)),
        ],
        out_specs=pl.BlockSpec((rows, tn), lambda j: (0, j)),
        compiler_params=_params(1),
        name="ada",
    )(c_pad, ada_w, ada_b.reshape(1, n_out))
    return out[:b].reshape(b, 6, d)


def _rope_kernel(pos_ref, invf_ref, sign_ref, cos_ref, sin_ref):
    ang = pos_ref[...].astype(F32) * invf_ref[...]
    cos_ref[...] = jnp.cos(ang)
    sin_ref[...] = jnp.sin(ang) * sign_ref[...]


def _rope_tables(pos_flat, tm):
    n = pos_flat.shape[0]
    half = HEAD_DIM // 2
    inv_freq = ROPE_THETA ** (-jnp.arange(0, HEAD_DIM, 2, dtype=F32) / HEAD_DIM)
    invf = jnp.concatenate([inv_freq, inv_freq]).reshape(1, HEAD_DIM)
    sign = jnp.concatenate([-jnp.ones((half,), F32), jnp.ones((half,), F32)]).reshape(1, HEAD_DIM)
    pos_b = jnp.broadcast_to(pos_flat[:, None], (n, HEAD_DIM))
    return pl.pallas_call(
        _rope_kernel,
        out_shape=(jax.ShapeDtypeStruct((n, HEAD_DIM), F32),) * 2,
        grid=(n // tm,),
        in_specs=[
            pl.BlockSpec((tm, HEAD_DIM), lambda i: (i, 0)),
            pl.BlockSpec((1, HEAD_DIM), lambda i: (0, 0)),
            pl.BlockSpec((1, HEAD_DIM), lambda i: (0, 0)),
        ],
        out_specs=(pl.BlockSpec((tm, HEAD_DIM), lambda i: (i, 0)),) * 2,
        compiler_params=_params(1),
        name="rope",
    )(pos_b, invf, sign)


def _conv_in_kernel(x_ref, ada_ref, g_ref, w_ref, o_ref):
    h = _mod_norm(x_ref[...], g_ref[...], ada_ref[0], 0, 1).astype(BF16)
    u = jnp.dot(h, w_ref[...], preferred_element_type=F32)
    half = u.shape[1] // 2
    o_ref[...] = (u[:, :half] * _sigmoid(u[:, half:])).astype(BF16)


def _qkv_kernel(x_ref, ada_ref, g_ref, wq_ref, wk_ref, wv_ref, qg_ref, kg_ref,
                cos_ref, sin_ref, q_ref, k_ref, v_ref, *, q_scale):
    h = _mod_norm(x_ref[...], g_ref[...], ada_ref[0], 0, 1).astype(BF16)
    cos = cos_ref[...]
    sin = sin_ref[...]

    def norm_rope(t, g, mul):
        y = t * _rms(t, HEAD_DIM) * g
        rot = pltpu.roll(y, HEAD_DIM // 2, 1)
        return (y * cos + rot * sin) * mul

    n_heads = q_ref.shape[1] // V_HEAD_DIM
    for w_ref_, g_ref_, o_ref_, mul in ((wq_ref, qg_ref, q_ref, q_scale),
                                        (wk_ref, kg_ref, k_ref, 1.0)):
        g = g_ref_[...]
        for hd in range(n_heads):
            c0 = hd * V_HEAD_DIM
            u = jnp.dot(h, w_ref_[:, c0:c0 + V_HEAD_DIM], preferred_element_type=F32)
            for m in range(2):
                t = u[:, m * HEAD_DIM:(m + 1) * HEAD_DIM]
                o_ref_[:, c0 + m * HEAD_DIM:c0 + (m + 1) * HEAD_DIM] = (
                    norm_rope(t, g, mul).astype(BF16))
    v_ref[...] = jnp.dot(h, wv_ref[...], preferred_element_type=F32).astype(BF16)


def _gates_kernel(x_ref, ada_ref, g_ref, w_ref, b_ref, o_ref, *, col_chunk):
    h = _mod_norm(x_ref[...], g_ref[...], ada_ref[0], 0, 1).astype(BF16)
    for c0 in range(0, o_ref.shape[1], col_chunk):
        u = jnp.dot(h, w_ref[:, c0:c0 + col_chunk], preferred_element_type=F32)
        o_ref[:, c0:c0 + col_chunk] = _sigmoid(u + b_ref[:, c0:c0 + col_chunk]).astype(BF16)


def _row_spec(tm, width):
    return pl.BlockSpec((tm, width), lambda i: (i, 0))


def _full_spec(shape):
    return pl.BlockSpec(shape, lambda i: (0,) * len(shape))


def _ada_spec(d, tiles_per_batch):
    return pl.BlockSpec((1, 6, d), lambda i: (i // tiles_per_batch, 0, 0))


def _conv_kernel(cur_ref, halo_ref, w_ref, b_ref, g_ref, o_ref, win_ref, *, rows_per_chunk):
    ts = cur_ref.shape[1]
    width = cur_ref.shape[2]
    halo = halo_ref[0].astype(F32)
    win_ref[0:CONV_HALO_ROWS, :] = jnp.where(pl.program_id(1) == 0, 0.0, halo)
    win_ref[CONV_HALO_ROWS:CONV_HALO_ROWS + ts, :] = cur_ref[0].astype(F32)
    first = CONV_HALO_ROWS - (CONV_KERNEL - 1)
    bias = b_ref[...]
    g = g_ref[...]
    for r0 in range(0, ts, rows_per_chunk):
        acc = jnp.zeros((rows_per_chunk, width), F32)
        for k in range(CONV_KERNEL):
            acc = acc + win_ref[first + r0 + k:first + r0 + k + rows_per_chunk, :] * w_ref[k:k + 1, :]
        y = acc + bias
        z = y * _rms(y, width) * g
        o_ref[0, r0:r0 + rows_per_chunk, :] = (z * _sigmoid(z)).astype(BF16)


def _conv_branch(glu, conv_w, conv_b, conv_norm_g, ts):
    b, s, width = glu.shape
    halo_per_tile = ts // CONV_HALO_ROWS
    return pl.pallas_call(
        functools.partial(_conv_kernel, rows_per_chunk=32),
        out_shape=jax.ShapeDtypeStruct((b, s, width), BF16),
        grid=(b, s // ts),
        in_specs=[
            pl.BlockSpec((1, ts, width), lambda bi, i: (bi, i, 0)),
            pl.BlockSpec((1, CONV_HALO_ROWS, width),
                         lambda bi, i: (bi, jnp.maximum(i * halo_per_tile - 1, 0), 0)),
            pl.BlockSpec((CONV_KERNEL, width), lambda bi, i: (0, 0)),
            pl.BlockSpec((1, width), lambda bi, i: (0, 0)),
            pl.BlockSpec((1, width), lambda bi, i: (0, 0)),
        ],
        out_specs=pl.BlockSpec((1, ts, width), lambda bi, i: (bi, i, 0)),
        scratch_shapes=[pltpu.VMEM((CONV_HALO_ROWS + ts, width), F32)],
        compiler_params=_params(2),
        name="conv",
    )(glu, glu, conv_w, conv_b.reshape(1, width), conv_norm_g.reshape(1, width))


def _attn_kernel(q_ref, k_ref, v_ref, lq1_ref, lk1_ref, lq2_ref, lk2_ref, sg_ref, o_ref,
                 m_ref, l_ref, acc_ref, *, lam_init):
    tq = q_ref.shape[0]
    tk = tq
    qi = pl.program_id(2)

    m_ref[...] = jnp.full(m_ref.shape, -jnp.inf, F32)
    l_ref[...] = jnp.zeros(l_ref.shape, F32)
    acc_ref[...] = jnp.zeros(acc_ref.shape, F32)

    def kv_step(j, mask):
        row0 = pl.multiple_of(j * tk, tk)
        k = k_ref[pl.ds(row0, tk), :]
        v = v_ref[pl.ds(row0, tk), :]
        for m in range(2):
            lanes = slice(m * HEAD_DIM, (m + 1) * HEAD_DIM)
            s = lax.dot_general(q_ref[:, lanes], k[:, lanes], (((1,), (1,)), ((), ())),
                                preferred_element_type=F32)
            if mask is not None:
                s = jnp.where(mask, s, -jnp.inf)
            m_old = m_ref[m]
            m_new = jnp.maximum(m_old, jnp.max(s, axis=-1, keepdims=True))
            alpha = jnp.exp2(m_old - m_new)
            p = jnp.exp2(s - m_new[:, 0:1])
            l_ref[m] = alpha * l_ref[m] + jnp.sum(p, axis=-1, keepdims=True)
            acc_ref[m] = alpha[:, 0:1] * acc_ref[m] + jnp.dot(
                p.astype(BF16), v, preferred_element_type=F32)
            m_ref[m] = m_new

    def body(j, carry):
        kv_step(j, None)
        return carry

    lax.fori_loop(0, qi, body, 0)
    q_chunk = lax.broadcasted_iota(jnp.int32, (tq, tk), 0) // CHUNK
    k_chunk = lax.broadcasted_iota(jnp.int32, (tq, tk), 1) // CHUNK
    kv_step(qi, k_chunk <= q_chunk)

    lam = (jnp.exp(jnp.sum(lq1_ref[...] * lk1_ref[...], axis=-1, keepdims=True))
           - jnp.exp(jnp.sum(lq2_ref[...] * lk2_ref[...], axis=-1, keepdims=True))
           + lam_init)
    o = acc_ref[0] / l_ref[0][:, 0:1] - lam * (acc_ref[1] / l_ref[1][:, 0:1])
    o = o * _rms(o, V_HEAD_DIM) * sg_ref[...] * (1.0 - lam_init)
    o_ref[...] = o.astype(BF16)


def _attention(q, k, v, lq1, lk1, lq2, lk2, subln_g, batch, seq, tq, lam_init):
    n, width = q.shape
    n_heads = width // V_HEAD_DIM
    nq = seq // tq
    vec = lambda a: a.reshape(1, HEAD_DIM)
    vec_spec = pl.BlockSpec((1, HEAD_DIM), lambda b, h, i: (0, 0))
    return pl.pallas_call(
        functools.partial(_attn_kernel, lam_init=lam_init),
        out_shape=jax.ShapeDtypeStruct((n, width), BF16),
        grid=(batch, n_heads, nq),
        in_specs=[
            pl.BlockSpec((tq, V_HEAD_DIM), lambda b, h, i: (b * nq + i, h)),
            pl.BlockSpec((seq, V_HEAD_DIM), lambda b, h, i: (b, h)),
            pl.BlockSpec((seq, V_HEAD_DIM), lambda b, h, i: (b, h)),
            vec_spec, vec_spec, vec_spec, vec_spec,
            pl.BlockSpec((1, V_HEAD_DIM), lambda b, h, i: (0, 0)),
        ],
        out_specs=pl.BlockSpec((tq, V_HEAD_DIM), lambda b, h, i: (b * nq + i, h)),
        scratch_shapes=[
            pltpu.VMEM((2, tq, HEAD_DIM), F32),
            pltpu.VMEM((2, tq, HEAD_DIM), F32),
            pltpu.VMEM((2, tq, V_HEAD_DIM), F32),
        ],
        compiler_params=_params(3),
        name="attn",
    )(q, k, v, vec(lq1), vec(lk1), vec(lq2), vec(lk2), subln_g.reshape(1, V_HEAD_DIM))


def _merge_kernel(x_ref, ada_ref, conv_ref, attn_ref, gates_ref, wc_ref, wa_ref, wo_ref, o_ref):
    d = x_ref.shape[1]
    y_conv = jnp.dot(conv_ref[...], wc_ref[...], preferred_element_type=F32)
    y_attn = jnp.dot(attn_ref[...], wa_ref[...], preferred_element_type=F32)
    merged = (gates_ref[:, :d].astype(F32) * y_conv + gates_ref[:, d:].astype(F32) * y_attn)
    out = jnp.dot(merged.astype(BF16), wo_ref[...], preferred_element_type=F32)
    o_ref[...] = x_ref[...] + ada_ref[0][2:3, :] * out


def _mlp_kernel(x_ref, ada_ref, g_ref, w1_ref, w2_ref, o_ref, h_ref, acc_ref):
    f = pl.program_id(1)

    @pl.when(f == 0)
    def _():
        h_ref[...] = _mod_norm(x_ref[...], g_ref[...], ada_ref[0], 3, 4).astype(BF16)
        acc_ref[...] = jnp.zeros(acc_ref.shape, F32)

    a = jnp.dot(h_ref[...], w1_ref[...], preferred_element_type=F32)
    a = jnp.square(jnp.maximum(a, 0.0)).astype(BF16)
    acc_ref[...] += jnp.dot(a, w2_ref[...], preferred_element_type=F32)

    @pl.when(f == pl.num_programs(1) - 1)
    def _():
        o_ref[...] = x_ref[...] + ada_ref[0][5:6, :] * acc_ref[...]


def _layer(x2, ada, cos, sin, batch, seq, lam_init, norm_mix_g, w_in, conv_w, conv_b, conv_norm_g,
           w_conv_out, q_norm_g, k_norm_g, lq1, lk1, lq2, lk2, subln_g, w_attn_out, gate_b, w_out,
           norm_mlp_g, w_mlp_in, w_mlp_out):
    n, d = x2.shape
    t = _tiles(seq)
    conv_width = conv_w.shape[1]
    qk_width = w_attn_out.shape[0]
    d_ff = w_mlp_in.shape[1]
    c_qk = 2 * conv_width
    w_glu = w_in[:, :c_qk].astype(BF16)
    w_q = w_in[:, c_qk:c_qk + qk_width].astype(BF16)
    w_k = w_in[:, c_qk + qk_width:c_qk + 2 * qk_width].astype(BF16)
    w_v = w_in[:, c_qk + 2 * qk_width:c_qk + 3 * qk_width].astype(BF16)
    w_gate = w_in[:, c_qk + 3 * qk_width:].astype(BF16)
    g_mix = norm_mix_g.reshape(1, d)

    tm = t["proj"]
    tpb = seq // tm
    glu = pl.pallas_call(
        _conv_in_kernel,
        out_shape=jax.ShapeDtypeStruct((n, conv_width), BF16),
        grid=(n // tm,),
        in_specs=[_row_spec(tm, d), _ada_spec(d, tpb), _full_spec((1, d)),
                  _full_spec((d, c_qk))],
        out_specs=_row_spec(tm, conv_width),
        compiler_params=_params(1),
        name="conv_in",
    )(x2, ada, g_mix, w_glu)

    q_scale = LOG2E / math.sqrt(HEAD_DIM)
    q, k, v = pl.pallas_call(
        functools.partial(_qkv_kernel, q_scale=q_scale),
        out_shape=(jax.ShapeDtypeStruct((n, qk_width), BF16),) * 3,
        grid=(n // tm,),
        in_specs=[_row_spec(tm, d), _ada_spec(d, tpb), _full_spec((1, d)),
                  _full_spec((d, qk_width)), _full_spec((d, qk_width)), _full_spec((d, qk_width)),
                  _full_spec((1, HEAD_DIM)), _full_spec((1, HEAD_DIM)),
                  _row_spec(tm, HEAD_DIM), _row_spec(tm, HEAD_DIM)],
        out_specs=(_row_spec(tm, qk_width),) * 3,
        compiler_params=_params(1),
        name="qkv",
    )(x2, ada, g_mix, w_q, w_k, w_v, q_norm_g.reshape(1, HEAD_DIM), k_norm_g.reshape(1, HEAD_DIM),
      cos, sin)

    gates = pl.pallas_call(
        functools.partial(_gates_kernel, col_chunk=min(1024, 2 * d)),
        out_shape=jax.ShapeDtypeStruct((n, 2 * d), BF16),
        grid=(n // tm,),
        in_specs=[_row_spec(tm, d), _ada_spec(d, tpb), _full_spec((1, d)),
                  _full_spec((d, 2 * d)), _full_spec((1, 2 * d))],
        out_specs=_row_spec(tm, 2 * d),
        compiler_params=_params(1),
        name="gates",
    )(x2, ada, g_mix, w_gate, gate_b.reshape(1, 2 * d))

    conv_act = _conv_branch(glu.reshape(batch, seq, conv_width), conv_w, conv_b, conv_norm_g,
                            t["conv"]).reshape(n, conv_width)
    attn = _attention(q, k, v, lq1, lk1, lq2, lk2, subln_g, batch, seq, t["attn"], lam_init)

    tm = t["merge"]
    x_mid = pl.pallas_call(
        _merge_kernel,
        out_shape=jax.ShapeDtypeStruct((n, d), F32),
        grid=(n // tm,),
        in_specs=[_row_spec(tm, d), _ada_spec(d, seq // tm), _row_spec(tm, conv_width),
                  _row_spec(tm, qk_width), _row_spec(tm, 2 * d),
                  _full_spec((conv_width, d)), _full_spec((qk_width, d)), _full_spec((d, d))],
        out_specs=_row_spec(tm, d),
        compiler_params=_params(1),
        name="merge",
    )(x2, ada, conv_act, attn, gates, w_conv_out.astype(BF16), w_attn_out.astype(BF16),
      w_out.astype(BF16))

    tm = t["mlp"]
    tf = min(t["mlp_ff"], d_ff)
    tpb = seq // tm
    return pl.pallas_call(
        _mlp_kernel,
        out_shape=jax.ShapeDtypeStruct((n, d), F32),
        grid=(n // tm, d_ff // tf),
        in_specs=[pl.BlockSpec((tm, d), lambda i, f: (i, 0)),
                  pl.BlockSpec((1, 6, d), lambda i, f: (i // tpb, 0, 0)),
                  pl.BlockSpec((1, d), lambda i, f: (0, 0)),
                  pl.BlockSpec((d, tf), lambda i, f: (0, f)),
                  pl.BlockSpec((tf, d), lambda i, f: (f, 0))],
        out_specs=pl.BlockSpec((tm, d), lambda i, f: (i, 0)),
        scratch_shapes=[pltpu.VMEM((tm, d), BF16), pltpu.VMEM((tm, d), F32)],
        compiler_params=_params(2),
        name="mlp",
    )(x_mid, ada, norm_mlp_g.reshape(1, d), w_mlp_in.astype(BF16), w_mlp_out.astype(BF16))


def kernel(x, c, pos, ada_w, ada_b, norm_mix_g, w_in, conv_w, conv_b, conv_norm_g, w_conv_out,
           q_norm_g, k_norm_g, lambda_q1, lambda_k1, lambda_q2, lambda_k2, subln_g, w_attn_out,
           gate_b, w_out, norm_mlp_g, w_mlp_in, w_mlp_out):
    batch, seq, d = x.shape
    depth = ada_w.shape[0]
    t = _tiles(seq)
    cos, sin = _rope_tables(pos.reshape(batch * seq), t["rope"])
    x2 = x.reshape(batch * seq, d)
    for l in range(depth):
        lam_init = 0.8 - 0.6 * math.exp(-0.3 * l)
        ada = _ada(c, ada_w[l], ada_b[l])
        x2 = _layer(x2, ada, cos, sin, batch, seq, lam_init, norm_mix_g[l], w_in[l], conv_w[l],
                    conv_b[l], conv_norm_g[l], w_conv_out[l], q_norm_g[l], k_norm_g[l],
                    lambda_q1[l], lambda_k1[l], lambda_q2[l], lambda_k2[l], subln_g[l],
                    w_attn_out[l], gate_b[l], w_out[l], norm_mlp_g[l], w_mlp_in[l], w_mlp_out[l])
    return x2.reshape(batch, seq, d)
```

```python
import functools
import math

import jax
import jax.numpy as jnp
from jax import lax
from jax.experimental import pallas as pl
from jax.experimental.pallas import tpu as pltpu

F32 = jnp.float32
BF16 = jnp.bfloat16

CHUNK = 64
CONV_KERNEL = 31
HEAD_DIM = 128
V_HEAD_DIM = 2 * HEAD_DIM
ROPE_THETA = 10000.0
EPS = 1e-6
LOG2E = math.log2(math.e)

V7X_VMEM_LIMIT_BYTES = 56 * 1024 * 1024
CONV_HALO_ROWS = 32
F32_SUBLANES = 8
ATTN_KV_UNROLL = 4


def _tiles(seq):
    return dict(
        rope=min(1024, seq),
        proj=min(512, seq),
        conv=min(512, seq),
        attn=min(512, seq),
        merge=min(256, seq),
        mlp=min(512, seq),
        mlp_ff=1024,
    )


def _params(n_axes):
    return pltpu.CompilerParams(
        dimension_semantics=("arbitrary",) * n_axes,
        vmem_limit_bytes=V7X_VMEM_LIMIT_BYTES,
    )


def _sigmoid(v):
    return 1.0 / (1.0 + jnp.exp(-v))


def _rms(v, axis_size):
    return lax.rsqrt(jnp.sum(v * v, axis=-1, keepdims=True) * (1.0 / axis_size) + EPS)


def _mod_norm(x, g, ada, shift_row, scale_row):
    y = x * _rms(x, x.shape[-1]) * g
    return y * (1.0 + ada[scale_row:scale_row + 1, :]) + ada[shift_row:shift_row + 1, :]


def _ada_kernel(c_ref, w_ref, b_ref, o_ref):
    c = c_ref[...]
    c_act = c * _sigmoid(c)
    o_ref[...] = jnp.dot(c_act.astype(BF16), w_ref[...].astype(BF16),
                         preferred_element_type=F32) + b_ref[...]


def _ada(c, ada_w, ada_b):
    b, d = c.shape
    n_out = ada_w.shape[1]
    rows = 8
    c_pad = jnp.zeros((rows, d), F32).at[:b].set(c)
    tn = 1536 if n_out % 1536 == 0 else n_out
    out = pl.pallas_call(
        _ada_kernel,
        out_shape=jax.ShapeDtypeStruct((rows, n_out), F32),
        grid=(n_out // tn,),
        in_specs=[
            pl.BlockSpec((rows, d), lambda j: (0, 0)),
            pl.BlockSpec((d, tn), lambda j: (0, j)),
            pl.BlockSpec((1, tn), lambda j: (0, j)),
        ],
        out_specs=pl.BlockSpec((rows, tn), lambda j: (0, j)),
        compiler_params=_params(1),
        name="ada",
    )(c_pad, ada_w, ada_b.reshape(1, n_out))
    return out[:b].reshape(b, 6, d)


def _rope_kernel(pos_ref, invf_ref, sign_ref, cos_ref, sin_ref):
    ang = pos_ref[...].astype(F32) * invf_ref[...]
    cos_ref[...] = jnp.cos(ang)
    sin_ref[...] = jnp.sin(ang) * sign_ref[...]


def _rope_tables(pos_flat, tm):
    n = pos_flat.shape[0]
    half = HEAD_DIM // 2
    inv_freq = ROPE_THETA ** (-jnp.arange(0, HEAD_DIM, 2, dtype=F32) / HEAD_DIM)
    invf = jnp.concatenate([inv_freq, inv_freq]).reshape(1, HEAD_DIM)
    sign = jnp.concatenate([-jnp.ones((half,), F32), jnp.ones((half,), F32)]).reshape(1, HEAD_DIM)
    pos_b = jnp.broadcast_to(pos_flat[:, None], (n, HEAD_DIM))
    return pl.pallas_call(
        _rope_kernel,
        out_shape=(jax.ShapeDtypeStruct((n, HEAD_DIM), F32),) * 2,
        grid=(n // tm,),
        in_specs=[
            pl.BlockSpec((tm, HEAD_DIM), lambda i: (i, 0)),
            pl.BlockSpec((1, HEAD_DIM), lambda i: (0, 0)),
            pl.BlockSpec((1, HEAD_DIM), lambda i: (0, 0)),
        ],
        out_specs=(pl.BlockSpec((tm, HEAD_DIM), lambda i: (i, 0)),) * 2,
        compiler_params=_params(1),
        name="rope",
    )(pos_b, invf, sign)


def _conv_in_kernel(x_ref, ada_ref, g_ref, w_ref, o_ref):
    h = _mod_norm(x_ref[...], g_ref[...], ada_ref[0], 0, 1).astype(BF16)
    u = jnp.dot(h, w_ref[...], preferred_element_type=F32)
    half = u.shape[1] // 2
    o_ref[...] = (u[:, :half] * _sigmoid(u[:, half:])).astype(BF16)


def _qkv_kernel(x_ref, ada_ref, g_ref, wq_ref, wk_ref, wv_ref, qg_ref, kg_ref,
                cos_ref, sin_ref, q_ref, k_ref, v_ref, *, q_scale):
    h = _mod_norm(x_ref[...], g_ref[...], ada_ref[0], 0, 1).astype(BF16)
    cos = cos_ref[...]
    sin = sin_ref[...]

    def norm_rope(t, g, mul):
        y = t * _rms(t, HEAD_DIM) * g
        rot = pltpu.roll(y, HEAD_DIM // 2, 1)
        return (y * cos + rot * sin) * mul

    n_heads = q_ref.shape[1] // V_HEAD_DIM
    for w_ref_, g_ref_, o_ref_, mul in ((wq_ref, qg_ref, q_ref, q_scale),
                                        (wk_ref, kg_ref, k_ref, 1.0)):
        g = g_ref_[...]
        for hd in range(n_heads):
            c0 = hd * V_HEAD_DIM
            u = jnp.dot(h, w_ref_[:, c0:c0 + V_HEAD_DIM], preferred_element_type=F32)
            for m in range(2):
                t = u[:, m * HEAD_DIM:(m + 1) * HEAD_DIM]
                o_ref_[:, c0 + m * HEAD_DIM:c0 + (m + 1) * HEAD_DIM] = (
                    norm_rope(t, g, mul).astype(BF16))
    v_ref[...] = jnp.dot(h, wv_ref[...], preferred_element_type=F32).astype(BF16)


def _gates_kernel(x_ref, ada_ref, g_ref, w_ref, b_ref, o_ref, *, col_chunk):
    h = _mod_norm(x_ref[...], g_ref[...], ada_ref[0], 0, 1).astype(BF16)
    for c0 in range(0, o_ref.shape[1], col_chunk):
        u = jnp.dot(h, w_ref[:, c0:c0 + col_chunk], preferred_element_type=F32)
        o_ref[:, c0:c0 + col_chunk] = _sigmoid(u + b_ref[:, c0:c0 + col_chunk]).astype(BF16)


def _row_spec(tm, width):
    return pl.BlockSpec((tm, width), lambda i: (i, 0))


def _full_spec(shape):
    return pl.BlockSpec(shape, lambda i: (0,) * len(shape))


def _ada_spec(d, tiles_per_batch):
    return pl.BlockSpec((1, 6, d), lambda i: (i // tiles_per_batch, 0, 0))


def _conv_kernel(cur_ref, halo_ref, w_ref, b_ref, g_ref, o_ref, win_ref, *, rows_per_chunk):
    ts = cur_ref.shape[1]
    width = cur_ref.shape[2]
    halo = halo_ref[0].astype(F32)
    win_ref[0, 0:CONV_HALO_ROWS, :] = jnp.where(pl.program_id(1) == 0, 0.0, halo)
    win_ref[0, CONV_HALO_ROWS:CONV_HALO_ROWS + ts, :] = cur_ref[0].astype(F32)
    shifted_rows = ts + CONV_HALO_ROWS - F32_SUBLANES
    for o in range(1, F32_SUBLANES):
        for r0 in range(0, shifted_rows, rows_per_chunk):
            rows = min(rows_per_chunk, shifted_rows - r0)
            win_ref[o, r0:r0 + rows, :] = win_ref[0, r0 + o:r0 + o + rows, :]
    first = CONV_HALO_ROWS - (CONV_KERNEL - 1)
    bias = b_ref[...]
    g = g_ref[...]

    def chunk(ci, carry):
        r0 = pl.multiple_of(ci * rows_per_chunk, rows_per_chunk)
        acc = None
        for k in range(CONV_KERNEL):
            o = (first + k) % F32_SUBLANES
            base = (first + k) - o
            term = win_ref[o, pl.ds(r0 + base, rows_per_chunk), :] * w_ref[k:k + 1, :]
            acc = term if acc is None else acc + term
        y = acc + bias
        z = y * _rms(y, width) * g
        o_ref[0, pl.ds(r0, rows_per_chunk), :] = (z * _sigmoid(z)).astype(BF16)
        return carry

    lax.fori_loop(0, ts // rows_per_chunk, chunk, 0)


def _conv_branch(glu, conv_w, conv_b, conv_norm_g, ts):
    b, s, width = glu.shape
    halo_per_tile = ts // CONV_HALO_ROWS
    return pl.pallas_call(
        functools.partial(_conv_kernel, rows_per_chunk=32),
        out_shape=jax.ShapeDtypeStruct((b, s, width), BF16),
        grid=(b, s // ts),
        in_specs=[
            pl.BlockSpec((1, ts, width), lambda bi, i: (bi, i, 0)),
            pl.BlockSpec((1, CONV_HALO_ROWS, width),
                         lambda bi, i: (bi, jnp.maximum(i * halo_per_tile - 1, 0), 0)),
            pl.BlockSpec((CONV_KERNEL, width), lambda bi, i: (0, 0)),
            pl.BlockSpec((1, width), lambda bi, i: (0, 0)),
            pl.BlockSpec((1, width), lambda bi, i: (0, 0)),
        ],
        out_specs=pl.BlockSpec((1, ts, width), lambda bi, i: (bi, i, 0)),
        scratch_shapes=[pltpu.VMEM((F32_SUBLANES, CONV_HALO_ROWS + ts, width), F32)],
        compiler_params=_params(2),
        name="conv",
    )(glu, glu, conv_w, conv_b.reshape(1, width), conv_norm_g.reshape(1, width))


def _attn_kernel(q_ref, k_ref, v_ref, lq1_ref, lk1_ref, lq2_ref, lk2_ref, sg_ref, o_ref,
                 m_ref, l_ref, acc_ref, *, lam_init, unroll):
    tq = q_ref.shape[0]
    tk = tq
    qi = pl.program_id(2)

    m_ref[...] = jnp.full(m_ref.shape, -jnp.inf, F32)
    l_ref[...] = jnp.zeros(l_ref.shape, F32)
    acc_ref[...] = jnp.zeros(acc_ref.shape, F32)

    def kv_tiles(tiles, mask):
        for j in tiles:
            row0 = pl.multiple_of(j * tk, tk)
            k = k_ref[pl.ds(row0, tk), :]
            v = v_ref[pl.ds(row0, tk), :]
            for m in range(2):
                lanes = slice(m * HEAD_DIM, (m + 1) * HEAD_DIM)
                s = lax.dot_general(q_ref[:, lanes], k[:, lanes], (((1,), (1,)), ((), ())),
                                    preferred_element_type=F32)
                if mask is not None:
                    s = jnp.where(mask, s, -jnp.inf)
                m_old = m_ref[m]
                m_new = jnp.maximum(m_old, jnp.max(s, axis=-1, keepdims=True))
                alpha = jnp.exp2(m_old - m_new)
                p = jnp.exp2(s - pltpu.repeat(m_new, tk // HEAD_DIM, axis=1))
                l_ref[m] = alpha * l_ref[m] + jnp.sum(p, axis=-1, keepdims=True)
                acc_ref[m] = (pltpu.repeat(alpha, V_HEAD_DIM // HEAD_DIM, axis=1) * acc_ref[m]
                              + jnp.dot(p.astype(BF16), v, preferred_element_type=F32))
                m_ref[m] = m_new

    def unrolled_body(jj, carry):
        kv_tiles([jj * unroll + u for u in range(unroll)], None)
        return carry

    def single_body(j, carry):
        kv_tiles([j], None)
        return carry

    n_unrolled = qi // unroll
    lax.fori_loop(0, n_unrolled, unrolled_body, 0)
    lax.fori_loop(n_unrolled * unroll, qi, single_body, 0)
    q_chunk = lax.broadcasted_iota(jnp.int32, (tq, tk), 0) // CHUNK
    k_chunk = lax.broadcasted_iota(jnp.int32, (tq, tk), 1) // CHUNK
    kv_tiles([qi], k_chunk <= q_chunk)

    lam = (jnp.exp(jnp.sum(lq1_ref[...] * lk1_ref[...], axis=-1, keepdims=True))
           - jnp.exp(jnp.sum(lq2_ref[...] * lk2_ref[...], axis=-1, keepdims=True))
           + lam_init)
    lanes_rep = V_HEAD_DIM // HEAD_DIM
    o = (acc_ref[0] * pltpu.repeat(1.0 / l_ref[0], lanes_rep, axis=1)
         - lam * (acc_ref[1] * pltpu.repeat(1.0 / l_ref[1], lanes_rep, axis=1)))
    o = o * _rms(o, V_HEAD_DIM) * sg_ref[...] * (1.0 - lam_init)
    o_ref[...] = o.astype(BF16)


def _attention(q, k, v, lq1, lk1, lq2, lk2, subln_g, batch, seq, tq, lam_init):
    n, width = q.shape
    n_heads = width // V_HEAD_DIM
    nq = seq // tq
    vec = lambda a: a.reshape(1, HEAD_DIM)
    vec_spec = pl.BlockSpec((1, HEAD_DIM), lambda b, h, i: (0, 0))
    return pl.pallas_call(
        functools.partial(_attn_kernel, lam_init=lam_init, unroll=ATTN_KV_UNROLL),
        out_shape=jax.ShapeDtypeStruct((n, width), BF16),
        grid=(batch, n_heads, nq),
        in_specs=[
            pl.BlockSpec((tq, V_HEAD_DIM), lambda b, h, i: (b * nq + i, h)),
            pl.BlockSpec((seq, V_HEAD_DIM), lambda b, h, i: (b, h)),
            pl.BlockSpec((seq, V_HEAD_DIM), lambda b, h, i: (b, h)),
            vec_spec, vec_spec, vec_spec, vec_spec,
            pl.BlockSpec((1, V_HEAD_DIM), lambda b, h, i: (0, 0)),
        ],
        out_specs=pl.BlockSpec((tq, V_HEAD_DIM), lambda b, h, i: (b * nq + i, h)),
        scratch_shapes=[
            pltpu.VMEM((2, tq, HEAD_DIM), F32),
            pltpu.VMEM((2, tq, HEAD_DIM), F32),
            pltpu.VMEM((2, tq, V_HEAD_DIM), F32),
        ],
        compiler_params=_params(3),
        name="attn",
    )(q, k, v, vec(lq1), vec(lk1), vec(lq2), vec(lk2), subln_g.reshape(1, V_HEAD_DIM))


def _merge_kernel(x_ref, ada_ref, conv_ref, attn_ref, gates_ref, wc_ref, wa_ref, wo_ref, g_ref,
                  o_ref, h_ref):
    d = x_ref.shape[1]
    ada = ada_ref[0]
    y_conv = jnp.dot(conv_ref[...], wc_ref[...], preferred_element_type=F32)
    y_attn = jnp.dot(attn_ref[...], wa_ref[...], preferred_element_type=F32)
    merged = (gates_ref[:, :d].astype(F32) * y_conv + gates_ref[:, d:].astype(F32) * y_attn)
    out = jnp.dot(merged.astype(BF16), wo_ref[...], preferred_element_type=F32)
    x_mid = x_ref[...] + ada[2:3, :] * out
    o_ref[...] = x_mid
    h_ref[...] = _mod_norm(x_mid, g_ref[...], ada, 3, 4).astype(BF16)


def _mlp_kernel(x_ref, h_ref, ada_ref, w1_ref, w2_ref, o_ref, acc_ref):
    f = pl.program_id(1)

    @pl.when(f == 0)
    def _():
        acc_ref[...] = jnp.zeros(acc_ref.shape, F32)

    a = jnp.dot(h_ref[...], w1_ref[...], preferred_element_type=F32)
    a = jnp.square(jnp.maximum(a, 0.0)).astype(BF16)
    acc_ref[...] += jnp.dot(a, w2_ref[...], preferred_element_type=F32)

    @pl.when(f == pl.num_programs(1) - 1)
    def _():
        o_ref[...] = x_ref[...] + ada_ref[0][5:6, :] * acc_ref[...]


def _layer(x2, ada, cos, sin, batch, seq, lam_init, norm_mix_g, w_in, conv_w, conv_b, conv_norm_g,
           w_conv_out, q_norm_g, k_norm_g, lq1, lk1, lq2, lk2, subln_g, w_attn_out, gate_b, w_out,
           norm_mlp_g, w_mlp_in, w_mlp_out):
    n, d = x2.shape
    t = _tiles(seq)
    conv_width = conv_w.shape[1]
    qk_width = w_attn_out.shape[0]
    d_ff = w_mlp_in.shape[1]
    c_qk = 2 * conv_width
    w_glu = w_in[:, :c_qk].astype(BF16)
    w_q = w_in[:, c_qk:c_qk + qk_width].astype(BF16)
    w_k = w_in[:, c_qk + qk_width:c_qk + 2 * qk_width].astype(BF16)
    w_v = w_in[:, c_qk + 2 * qk_width:c_qk + 3 * qk_width].astype(BF16)
    w_gate = w_in[:, c_qk + 3 * qk_width:].astype(BF16)
    g_mix = norm_mix_g.reshape(1, d)

    tm = t["proj"]
    tpb = seq // tm
    glu = pl.pallas_call(
        _conv_in_kernel,
        out_shape=jax.ShapeDtypeStruct((n, conv_width), BF16),
        grid=(n // tm,),
        in_specs=[_row_spec(tm, d), _ada_spec(d, tpb), _full_spec((1, d)),
                  _full_spec((d, c_qk))],
        out_specs=_row_spec(tm, conv_width),
        compiler_params=_params(1),
        name="conv_in",
    )(x2, ada, g_mix, w_glu)

    q_scale = LOG2E / math.sqrt(HEAD_DIM)
    q, k, v = pl.pallas_call(
        functools.partial(_qkv_kernel, q_scale=q_scale),
        out_shape=(jax.ShapeDtypeStruct((n, qk_width), BF16),) * 3,
        grid=(n // tm,),
        in_specs=[_row_spec(tm, d), _ada_spec(d, tpb), _full_spec((1, d)),
                  _full_spec((d, qk_width)), _full_spec((d, qk_width)), _full_spec((d, qk_width)),
                  _full_spec((1, HEAD_DIM)), _full_spec((1, HEAD_DIM)),
                  _row_spec(tm, HEAD_DIM), _row_spec(tm, HEAD_DIM)],
        out_specs=(_row_spec(tm, qk_width),) * 3,
        compiler_params=_params(1),
        name="qkv",
    )(x2, ada, g_mix, w_q, w_k, w_v, q_norm_g.reshape(1, HEAD_DIM), k_norm_g.reshape(1, HEAD_DIM),
      cos, sin)

    gates = pl.pallas_call(
        functools.partial(_gates_kernel, col_chunk=min(1024, 2 * d)),
        out_shape=jax.ShapeDtypeStruct((n, 2 * d), BF16),
        grid=(n // tm,),
        in_specs=[_row_spec(tm, d), _ada_spec(d, tpb), _full_spec((1, d)),
                  _full_spec((d, 2 * d)), _full_spec((1, 2 * d))],
        out_specs=_row_spec(tm, 2 * d),
        compiler_params=_params(1),
        name="gates",
    )(x2, ada, g_mix, w_gate, gate_b.reshape(1, 2 * d))

    conv_act = _conv_branch(glu.reshape(batch, seq, conv_width), conv_w, conv_b, conv_norm_g,
                            t["conv"]).reshape(n, conv_width)
    attn = _attention(q, k, v, lq1, lk1, lq2, lk2, subln_g, batch, seq, t["attn"], lam_init)

    tm = t["merge"]
    x_mid, h_mlp = pl.pallas_call(
        _merge_kernel,
        out_shape=(jax.ShapeDtypeStruct((n, d), F32), jax.ShapeDtypeStruct((n, d), BF16)),
        grid=(n // tm,),
        in_specs=[_row_spec(tm, d), _ada_spec(d, seq // tm), _row_spec(tm, conv_width),
                  _row_spec(tm, qk_width), _row_spec(tm, 2 * d),
                  _full_spec((conv_width, d)), _full_spec((qk_width, d)), _full_spec((d, d)),
                  _full_spec((1, d))],
        out_specs=(_row_spec(tm, d), _row_spec(tm, d)),
        compiler_params=_params(1),
        name="merge",
    )(x2, ada, conv_act, attn, gates, w_conv_out.astype(BF16), w_attn_out.astype(BF16),
      w_out.astype(BF16), norm_mlp_g.reshape(1, d))

    tm = t["mlp"]
    tf = min(t["mlp_ff"], d_ff)
    tpb = seq // tm
    return pl.pallas_call(
        _mlp_kernel,
        out_shape=jax.ShapeDtypeStruct((n, d), F32),
        grid=(n // tm, d_ff // tf),
        in_specs=[pl.BlockSpec((tm, d), lambda i, f: (i, 0)),
                  pl.BlockSpec((tm, d), lambda i, f: (i, 0)),
                  pl.BlockSpec((1, 6, d), lambda i, f: (i // tpb, 0, 0)),
                  pl.BlockSpec((d, tf), lambda i, f: (0, f)),
                  pl.BlockSpec((tf, d), lambda i, f: (f, 0))],
        out_specs=pl.BlockSpec((tm, d), lambda i, f: (i, 0)),
        scratch_shapes=[pltpu.VMEM((tm, d), F32)],
        compiler_params=_params(2),
        name="mlp",
    )(x_mid, h_mlp, ada, w_mlp_in.astype(BF16), w_mlp_out.astype(BF16))


def kernel(x, c, pos, ada_w, ada_b, norm_mix_g, w_in, conv_w, conv_b, conv_norm_g, w_conv_out,
           q_norm_g, k_norm_g, lambda_q1, lambda_k1, lambda_q2, lambda_k2, subln_g, w_attn_out,
           gate_b, w_out, norm_mlp_g, w_mlp_in, w_mlp_out):
    batch, seq, d = x.shape
    depth = ada_w.shape[0]
    t = _tiles(seq)
    cos, sin = _rope_tables(pos.reshape(batch * seq), t["rope"])
    x2 = x.reshape(batch * seq, d)
    for l in range(depth):
        lam_init = 0.8 - 0.6 * math.exp(-0.3 * l)
        ada = _ada(c, ada_w[l], ada_b[l])
        x2 = _layer(x2, ada, cos, sin, batch, seq, lam_init, norm_mix_g[l], w_in[l], conv_w[l],
                    conv_b[l], conv_norm_g[l], w_conv_out[l], q_norm_g[l], k_norm_g[l],
                    lambda_q1[l], lambda_k1[l], lambda_q2[l], lambda_k2[l], subln_g[l],
                    w_attn_out[l], gate_b[l], w_out[l], norm_mlp_g[l], w_mlp_in[l], w_mlp_out[l])
    return x2.reshape(batch, seq, d)
```

```python
import functools
import math

import jax
import jax.numpy as jnp
from jax import lax
from jax.experimental import pallas as pl
from jax.experimental.pallas import tpu as pltpu

F32 = jnp.float32
BF16 = jnp.bfloat16

CHUNK = 64
CONV_KERNEL = 31
HEAD_DIM = 128
V_HEAD_DIM = 2 * HEAD_DIM
ROPE_THETA = 10000.0
EPS = 1e-6
LOG2E = math.log2(math.e)

V7X_VMEM_LIMIT_BYTES = 56 * 1024 * 1024
CONV_HALO_ROWS = 32
F32_SUBLANES = 8
ATTN_KV_UNROLL = 4
PROJ_ROW_PARTS = 2


def _tiles(seq):
    return dict(
        rope=min(1024, seq),
        proj=min(512, seq),
        conv=min(512, seq),
        attn=min(512, seq),
        merge=min(512, seq),
        mlp=min(512, seq),
        mlp_ff=1024,
    )


def _params(n_axes):
    return pltpu.CompilerParams(
        dimension_semantics=("arbitrary",) * n_axes,
        vmem_limit_bytes=V7X_VMEM_LIMIT_BYTES,
    )


def _sigmoid(v):
    return 1.0 / (1.0 + jnp.exp(-v))


def _lane_tile(a, reps):
    return jnp.concatenate([a] * reps, axis=-1) if reps > 1 else a


def _rms(v, axis_size):
    return lax.rsqrt(jnp.sum(v * v, axis=-1, keepdims=True) * (1.0 / axis_size) + EPS)


def _mod_norm(x, g, ada, shift_row, scale_row):
    y = x * _rms(x, x.shape[-1]) * g
    return y * (1.0 + ada[scale_row:scale_row + 1, :]) + ada[shift_row:shift_row + 1, :]


def _ada_kernel(c_ref, w_ref, b_ref, o_ref):
    c = c_ref[...]
    c_act = c * _sigmoid(c)
    o_ref[...] = jnp.dot(c_act.astype(BF16), w_ref[...].astype(BF16),
                         preferred_element_type=F32) + b_ref[...]


def _ada(c, ada_w, ada_b):
    b, d = c.shape
    n_out = ada_w.shape[1]
    rows = 8
    c_pad = jnp.zeros((rows, d), F32).at[:b].set(c)
    tn = 1536 if n_out % 1536 == 0 else n_out
    out = pl.pallas_call(
        _ada_kernel,
        out_shape=jax.ShapeDtypeStruct((rows, n_out), F32),
        grid=(n_out // tn,),
        in_specs=[
            pl.BlockSpec((rows, d), lambda j: (0, 0)),
            pl.BlockSpec((d, tn), lambda j: (0, j)),
            pl.BlockSpec((1, tn), lambda j: (0, j)),
        ],
        out_specs=pl.BlockSpec((rows, tn), lambda j: (0, j)),
        compiler_params=_params(1),
        name="ada",
    )(c_pad, ada_w, ada_b.reshape(1, n_out))
    return out[:b].reshape(b, 6, d)


def _rope_kernel(pos_ref, invf_ref, sign_ref, cos_ref, sin_ref):
    ang = pos_ref[...].astype(F32) * invf_ref[...]
    cos_ref[...] = jnp.cos(ang)
    sin_ref[...] = jnp.sin(ang) * sign_ref[...]


def _rope_tables(pos_flat, tm):
    n = pos_flat.shape[0]
    half = HEAD_DIM // 2
    inv_freq = ROPE_THETA ** (-jnp.arange(0, HEAD_DIM, 2, dtype=F32) / HEAD_DIM)
    invf = jnp.concatenate([inv_freq, inv_freq]).reshape(1, HEAD_DIM)
    sign = jnp.concatenate([-jnp.ones((half,), F32), jnp.ones((half,), F32)]).reshape(1, HEAD_DIM)
    pos_b = jnp.broadcast_to(pos_flat[:, None], (n, HEAD_DIM))
    return pl.pallas_call(
        _rope_kernel,
        out_shape=(jax.ShapeDtypeStruct((n, HEAD_DIM), F32),) * 2,
        grid=(n // tm,),
        in_specs=[
            pl.BlockSpec((tm, HEAD_DIM), lambda i: (i, 0)),
            pl.BlockSpec((1, HEAD_DIM), lambda i: (0, 0)),
            pl.BlockSpec((1, HEAD_DIM), lambda i: (0, 0)),
        ],
        out_specs=(pl.BlockSpec((tm, HEAD_DIM), lambda i: (i, 0)),) * 2,
        compiler_params=_params(1),
        name="rope",
    )(pos_b, invf, sign)


def _row_parts(rows):
    part = rows // PROJ_ROW_PARTS if rows % (PROJ_ROW_PARTS * 16) == 0 else rows
    return [slice(r0, r0 + part) for r0 in range(0, rows, part)]


def _conv_in_kernel(x_ref, ada_ref, g_ref, w_ref, o_ref):
    for rows in _row_parts(x_ref.shape[0]):
        h = _mod_norm(x_ref[rows, :], g_ref[...], ada_ref[0], 0, 1).astype(BF16)
        u = jnp.dot(h, w_ref[...], preferred_element_type=F32)
        half = u.shape[1] // 2
        o_ref[rows, :] = (u[:, :half] * _sigmoid(u[:, half:])).astype(BF16)


def _qkv_kernel(x_ref, ada_ref, g_ref, wq_ref, wk_ref, wv_ref, qg_ref, kg_ref,
                cos_ref, sin_ref, q_ref, k_ref, v_ref, *, q_scale):
    n_heads = q_ref.shape[1] // V_HEAD_DIM
    for rows in _row_parts(x_ref.shape[0]):
        h = _mod_norm(x_ref[rows, :], g_ref[...], ada_ref[0], 0, 1).astype(BF16)
        cos = cos_ref[rows, :]
        sin = sin_ref[rows, :]

        def norm_rope(t, g, mul):
            y = t * _rms(t, HEAD_DIM) * g
            rot = pltpu.roll(y, HEAD_DIM // 2, 1)
            return (y * cos + rot * sin) * mul

        for w_ref_, g_ref_, o_ref_, mul in ((wq_ref, qg_ref, q_ref, q_scale),
                                            (wk_ref, kg_ref, k_ref, 1.0)):
            g = g_ref_[...]
            for hd in range(n_heads):
                c0 = hd * V_HEAD_DIM
                u = jnp.dot(h, w_ref_[:, c0:c0 + V_HEAD_DIM], preferred_element_type=F32)
                for m in range(2):
                    t = u[:, m * HEAD_DIM:(m + 1) * HEAD_DIM]
                    o_ref_[rows, c0 + m * HEAD_DIM:c0 + (m + 1) * HEAD_DIM] = (
                        norm_rope(t, g, mul).astype(BF16))
        v_ref[rows, :] = jnp.dot(h, wv_ref[...], preferred_element_type=F32).astype(BF16)


def _gates_kernel(x_ref, ada_ref, g_ref, w_ref, b_ref, o_ref, *, col_chunk):
    for rows in _row_parts(x_ref.shape[0]):
        h = _mod_norm(x_ref[rows, :], g_ref[...], ada_ref[0], 0, 1).astype(BF16)
        for c0 in range(0, o_ref.shape[1], col_chunk):
            u = jnp.dot(h, w_ref[:, c0:c0 + col_chunk], preferred_element_type=F32)
            o_ref[rows, c0:c0 + col_chunk] = _sigmoid(u + b_ref[:, c0:c0 + col_chunk]).astype(BF16)


def _row_spec(tm, width):
    return pl.BlockSpec((tm, width), lambda i: (i, 0))


def _full_spec(shape):
    return pl.BlockSpec(shape, lambda i: (0,) * len(shape))


def _ada_spec(d, tiles_per_batch):
    return pl.BlockSpec((1, 6, d), lambda i: (i // tiles_per_batch, 0, 0))


def _conv_kernel(cur_ref, halo_ref, w_ref, b_ref, g_ref, o_ref, win_ref, *, rows_per_chunk):
    ts = cur_ref.shape[1]
    width = cur_ref.shape[2]
    halo = halo_ref[0].astype(F32)
    win_ref[0, 0:CONV_HALO_ROWS, :] = jnp.where(pl.program_id(1) == 0, 0.0, halo)
    win_ref[0, CONV_HALO_ROWS:CONV_HALO_ROWS + ts, :] = cur_ref[0].astype(F32)
    shifted_rows = ts + CONV_HALO_ROWS - F32_SUBLANES
    for o in range(1, F32_SUBLANES):
        for r0 in range(0, shifted_rows, rows_per_chunk):
            rows = min(rows_per_chunk, shifted_rows - r0)
            win_ref[o, r0:r0 + rows, :] = win_ref[0, r0 + o:r0 + o + rows, :]
    first = CONV_HALO_ROWS - (CONV_KERNEL - 1)
    bias = b_ref[...]
    g = g_ref[...]

    def chunk(ci, carry):
        r0 = pl.multiple_of(ci * rows_per_chunk, rows_per_chunk)
        acc = None
        for k in range(CONV_KERNEL):
            o = (first + k) % F32_SUBLANES
            base = (first + k) - o
            term = win_ref[o, pl.ds(r0 + base, rows_per_chunk), :] * w_ref[k:k + 1, :]
            acc = term if acc is None else acc + term
        y = acc + bias
        z = y * _rms(y, width) * g
        o_ref[0, pl.ds(r0, rows_per_chunk), :] = (z * _sigmoid(z)).astype(BF16)
        return carry

    lax.fori_loop(0, ts // rows_per_chunk, chunk, 0, unroll=2)


def _conv_branch(glu, conv_w, conv_b, conv_norm_g, ts):
    b, s, width = glu.shape
    halo_per_tile = ts // CONV_HALO_ROWS
    return pl.pallas_call(
        functools.partial(_conv_kernel, rows_per_chunk=32),
        out_shape=jax.ShapeDtypeStruct((b, s, width), BF16),
        grid=(b, s // ts),
        in_specs=[
            pl.BlockSpec((1, ts, width), lambda bi, i: (bi, i, 0)),
            pl.BlockSpec((1, CONV_HALO_ROWS, width),
                         lambda bi, i: (bi, jnp.maximum(i * halo_per_tile - 1, 0), 0)),
            pl.BlockSpec((CONV_KERNEL, width), lambda bi, i: (0, 0)),
            pl.BlockSpec((1, width), lambda bi, i: (0, 0)),
            pl.BlockSpec((1, width), lambda bi, i: (0, 0)),
        ],
        out_specs=pl.BlockSpec((1, ts, width), lambda bi, i: (bi, i, 0)),
        scratch_shapes=[pltpu.VMEM((F32_SUBLANES, CONV_HALO_ROWS + ts, width), F32)],
        compiler_params=_params(2),
        name="conv",
    )(glu, glu, conv_w, conv_b.reshape(1, width), conv_norm_g.reshape(1, width))


def _attn_kernel(q_ref, k_ref, v_ref, lq1_ref, lk1_ref, lq2_ref, lk2_ref, sg_ref, o_ref,
                 m_ref, l_ref, acc_ref, s_ref, *, lam_init, unroll):
    tq = q_ref.shape[0]
    tk = tq
    qi = pl.program_id(2)

    m_ref[...] = jnp.full(m_ref.shape, -jnp.inf, F32)
    l_ref[...] = jnp.zeros(l_ref.shape, F32)
    acc_ref[...] = jnp.zeros(acc_ref.shape, F32)

    def kv_tiles(tiles, last_mask=None):
        for u, j in enumerate(tiles):
            row0 = pl.multiple_of(j * tk, tk)
            for m in range(2):
                lanes = slice(m * HEAD_DIM, (m + 1) * HEAD_DIM)
                s_ref[u, m] = lax.dot_general(
                    q_ref[:, lanes], k_ref[pl.ds(row0, tk), lanes], (((1,), (1,)), ((), ())),
                    preferred_element_type=F32)
        for u, j in enumerate(tiles):
            row0 = pl.multiple_of(j * tk, tk)
            v = v_ref[pl.ds(row0, tk), :]
            for m in range(2):
                s = s_ref[u, m]
                if last_mask is not None and u == len(tiles) - 1:
                    s = jnp.where(last_mask, s, -jnp.inf)
                m_old = m_ref[m]
                m_new = jnp.maximum(m_old, jnp.max(s, axis=-1, keepdims=True))
                alpha = jnp.exp2(m_old - m_new)
                p = jnp.exp2(s - _lane_tile(m_new, tk // HEAD_DIM))
                l_ref[m] = alpha * l_ref[m] + jnp.sum(p, axis=-1, keepdims=True)
                acc_ref[m] = (_lane_tile(alpha, V_HEAD_DIM // HEAD_DIM) * acc_ref[m]
                              + jnp.dot(p.astype(BF16), v, preferred_element_type=F32))
                m_ref[m] = m_new

    def unrolled_body(jj, carry):
        kv_tiles([jj * unroll + u for u in range(unroll)])
        return carry

    n_unrolled = qi // unroll
    lax.fori_loop(0, n_unrolled, unrolled_body, 0)
    q_chunk = lax.broadcasted_iota(jnp.int32, (tq, tk), 0) // CHUNK
    k_chunk = lax.broadcasted_iota(jnp.int32, (tq, tk), 1) // CHUNK
    diag_mask = k_chunk <= q_chunk
    first_rest = n_unrolled * unroll
    for rest in range(unroll):
        @pl.when(qi - first_rest == rest)
        def _(rest=rest):
            kv_tiles([first_rest + u for u in range(rest)] + [qi], diag_mask)

    lam = (jnp.exp(jnp.sum(lq1_ref[...] * lk1_ref[...], axis=-1, keepdims=True))
           - jnp.exp(jnp.sum(lq2_ref[...] * lk2_ref[...], axis=-1, keepdims=True))
           + lam_init)
    lanes_rep = V_HEAD_DIM // HEAD_DIM
    o = (acc_ref[0] * _lane_tile(1.0 / l_ref[0], lanes_rep)
         - lam * (acc_ref[1] * _lane_tile(1.0 / l_ref[1], lanes_rep)))
    o = o * _rms(o, V_HEAD_DIM) * sg_ref[...] * (1.0 - lam_init)
    o_ref[...] = o.astype(BF16)


def _attention(q, k, v, lq1, lk1, lq2, lk2, subln_g, batch, seq, tq, lam_init):
    n, width = q.shape
    n_heads = width // V_HEAD_DIM
    nq = seq // tq
    vec = lambda a: a.reshape(1, HEAD_DIM)
    vec_spec = pl.BlockSpec((1, HEAD_DIM), lambda b, h, i: (0, 0))
    return pl.pallas_call(
        functools.partial(_attn_kernel, lam_init=lam_init, unroll=ATTN_KV_UNROLL),
        out_shape=jax.ShapeDtypeStruct((n, width), BF16),
        grid=(batch, n_heads, nq),
        in_specs=[
            pl.BlockSpec((tq, V_HEAD_DIM), lambda b, h, i: (b * nq + i, h)),
            pl.BlockSpec((seq, V_HEAD_DIM), lambda b, h, i: (b, h)),
            pl.BlockSpec((seq, V_HEAD_DIM), lambda b, h, i: (b, h)),
            vec_spec, vec_spec, vec_spec, vec_spec,
            pl.BlockSpec((1, V_HEAD_DIM), lambda b, h, i: (0, 0)),
        ],
        out_specs=pl.BlockSpec((tq, V_HEAD_DIM), lambda b, h, i: (b * nq + i, h)),
        scratch_shapes=[
            pltpu.VMEM((2, tq, HEAD_DIM), F32),
            pltpu.VMEM((2, tq, HEAD_DIM), F32),
            pltpu.VMEM((2, tq, V_HEAD_DIM), F32),
            pltpu.VMEM((ATTN_KV_UNROLL, 2, tq, tq), F32),
        ],
        compiler_params=_params(3),
        name="attn",
    )(q, k, v, vec(lq1), vec(lk1), vec(lq2), vec(lk2), subln_g.reshape(1, V_HEAD_DIM))


def _merge_kernel(x_ref, ada_ref, conv_ref, attn_ref, gates_ref, wc_ref, wa_ref, wo_ref, g_ref,
                  o_ref, h_ref):
    d = x_ref.shape[1]
    ada = ada_ref[0]
    for rows in _row_parts(x_ref.shape[0]):
        y_conv = jnp.dot(conv_ref[rows, :], wc_ref[...], preferred_element_type=F32)
        y_attn = jnp.dot(attn_ref[rows, :], wa_ref[...], preferred_element_type=F32)
        merged = (gates_ref[rows, :d].astype(F32) * y_conv
                  + gates_ref[rows, d:].astype(F32) * y_attn)
        out = jnp.dot(merged.astype(BF16), wo_ref[...], preferred_element_type=F32)
        x_mid = x_ref[rows, :] + ada[2:3, :] * out
        o_ref[rows, :] = x_mid
        h_ref[rows, :] = _mod_norm(x_mid, g_ref[...], ada, 3, 4).astype(BF16)


def _mlp_kernel(x_ref, h_ref, ada_ref, w1_ref, w2_ref, o_ref, acc_ref):
    f = pl.program_id(1)

    @pl.when(f == 0)
    def _():
        acc_ref[...] = jnp.zeros(acc_ref.shape, F32)

    a = jnp.dot(h_ref[...], w1_ref[...], preferred_element_type=F32)
    a = jnp.square(jnp.maximum(a, 0.0)).astype(BF16)
    acc_ref[...] += jnp.dot(a, w2_ref[...], preferred_element_type=F32)

    @pl.when(f == pl.num_programs(1) - 1)
    def _():
        o_ref[...] = x_ref[...] + ada_ref[0][5:6, :] * acc_ref[...]


def _layer(x2, ada, cos, sin, batch, seq, lam_init, norm_mix_g, w_in, conv_w, conv_b, conv_norm_g,
           w_conv_out, q_norm_g, k_norm_g, lq1, lk1, lq2, lk2, subln_g, w_attn_out, gate_b, w_out,
           norm_mlp_g, w_mlp_in, w_mlp_out):
    n, d = x2.shape
    t = _tiles(seq)
    conv_width = conv_w.shape[1]
    qk_width = w_attn_out.shape[0]
    d_ff = w_mlp_in.shape[1]
    c_qk = 2 * conv_width
    w_glu = w_in[:, :c_qk].astype(BF16)
    w_q = w_in[:, c_qk:c_qk + qk_width].astype(BF16)
    w_k = w_in[:, c_qk + qk_width:c_qk + 2 * qk_width].astype(BF16)
    w_v = w_in[:, c_qk + 2 * qk_width:c_qk + 3 * qk_width].astype(BF16)
    w_gate = w_in[:, c_qk + 3 * qk_width:].astype(BF16)
    g_mix = norm_mix_g.reshape(1, d)

    tm = t["proj"]
    tpb = seq // tm
    glu = pl.pallas_call(
        _conv_in_kernel,
        out_shape=jax.ShapeDtypeStruct((n, conv_width), BF16),
        grid=(n // tm,),
        in_specs=[_row_spec(tm, d), _ada_spec(d, tpb), _full_spec((1, d)),
                  _full_spec((d, c_qk))],
        out_specs=_row_spec(tm, conv_width),
        compiler_params=_params(1),
        name="conv_in",
    )(x2, ada, g_mix, w_glu)

    q_scale = LOG2E / math.sqrt(HEAD_DIM)
    q, k, v = pl.pallas_call(
        functools.partial(_qkv_kernel, q_scale=q_scale),
        out_shape=(jax.ShapeDtypeStruct((n, qk_width), BF16),) * 3,
        grid=(n // tm,),
        in_specs=[_row_spec(tm, d), _ada_spec(d, tpb), _full_spec((1, d)),
                  _full_spec((d, qk_width)), _full_spec((d, qk_width)), _full_spec((d, qk_width)),
                  _full_spec((1, HEAD_DIM)), _full_spec((1, HEAD_DIM)),
                  _row_spec(tm, HEAD_DIM), _row_spec(tm, HEAD_DIM)],
        out_specs=(_row_spec(tm, qk_width),) * 3,
        compiler_params=_params(1),
        name="qkv",
    )(x2, ada, g_mix, w_q, w_k, w_v, q_norm_g.reshape(1, HEAD_DIM), k_norm_g.reshape(1, HEAD_DIM),
      cos, sin)

    gates = pl.pallas_call(
        functools.partial(_gates_kernel, col_chunk=min(1024, 2 * d)),
        out_shape=jax.ShapeDtypeStruct((n, 2 * d), BF16),
        grid=(n // tm,),
        in_specs=[_row_spec(tm, d), _ada_spec(d, tpb), _full_spec((1, d)),
                  _full_spec((d, 2 * d)), _full_spec((1, 2 * d))],
        out_specs=_row_spec(tm, 2 * d),
        compiler_params=_params(1),
        name="gates",
    )(x2, ada, g_mix, w_gate, gate_b.reshape(1, 2 * d))

    conv_act = _conv_branch(glu.reshape(batch, seq, conv_width), conv_w, conv_b, conv_norm_g,
                            t["conv"]).reshape(n, conv_width)
    attn = _attention(q, k, v, lq1, lk1, lq2, lk2, subln_g, batch, seq, t["attn"], lam_init)

    tm = t["merge"]
    x_mid, h_mlp = pl.pallas_call(
        _merge_kernel,
        out_shape=(jax.ShapeDtypeStruct((n, d), F32), jax.ShapeDtypeStruct((n, d), BF16)),
        grid=(n // tm,),
        in_specs=[_row_spec(tm, d), _ada_spec(d, seq // tm), _row_spec(tm, conv_width),
                  _row_spec(tm, qk_width), _row_spec(tm, 2 * d),
                  _full_spec((conv_width, d)), _full_spec((qk_width, d)), _full_spec((d, d)),
                  _full_spec((1, d))],
        out_specs=(_row_spec(tm, d), _row_spec(tm, d)),
        compiler_params=_params(1),
        name="merge",
    )(x2, ada, conv_act, attn, gates, w_conv_out.astype(BF16), w_attn_out.astype(BF16),
      w_out.astype(BF16), norm_mlp_g.reshape(1, d))

    tm = t["mlp"]
    tf = min(t["mlp_ff"], d_ff)
    tpb = seq // tm
    return pl.pallas_call(
        _mlp_kernel,
        out_shape=jax.ShapeDtypeStruct((n, d), F32),
        grid=(n // tm, d_ff // tf),
        in_specs=[pl.BlockSpec((tm, d), lambda i, f: (i, 0)),
                  pl.BlockSpec((tm, d), lambda i, f: (i, 0)),
                  pl.BlockSpec((1, 6, d), lambda i, f: (i // tpb, 0, 0)),
                  pl.BlockSpec((d, tf), lambda i, f: (0, f)),
                  pl.BlockSpec((tf, d), lambda i, f: (f, 0))],
        out_specs=pl.BlockSpec((tm, d), lambda i, f: (i, 0)),
        scratch_shapes=[pltpu.VMEM((tm, d), F32)],
        compiler_params=_params(2),
        name="mlp",
    )(x_mid, h_mlp, ada, w_mlp_in.astype(BF16), w_mlp_out.astype(BF16))


def kernel(x, c, pos, ada_w, ada_b, norm_mix_g, w_in, conv_w, conv_b, conv_norm_g, w_conv_out,
           q_norm_g, k_norm_g, lambda_q1, lambda_k1, lambda_q2, lambda_k2, subln_g, w_attn_out,
           gate_b, w_out, norm_mlp_g, w_mlp_in, w_mlp_out):
    batch, seq, d = x.shape
    depth = ada_w.shape[0]
    t = _tiles(seq)
    cos, sin = _rope_tables(pos.reshape(batch * seq), t["rope"])
    x2 = x.reshape(batch * seq, d)
    for l in range(depth):
        lam_init = 0.8 - 0.6 * math.exp(-0.3 * l)
        ada = _ada(c, ada_w[l], ada_b[l])
        x2 = _layer(x2, ada, cos, sin, batch, seq, lam_init, norm_mix_g[l], w_in[l], conv_w[l],
                    conv_b[l], conv_norm_g[l], w_conv_out[l], q_norm_g[l], k_norm_g[l],
                    lambda_q1[l], lambda_k1[l], lambda_q2[l], lambda_k2[l], subln_g[l],
                    w_attn_out[l], gate_b[l], w_out[l], norm_mlp_g[l], w_mlp_in[l], w_mlp_out[l])
    return x2.reshape(batch, seq, d)
```

```python
import functools
import math

import jax
import jax.numpy as jnp
from jax import lax
from jax.experimental import pallas as pl
from jax.experimental.pallas import tpu as pltpu

F32 = jnp.float32
BF16 = jnp.bfloat16

CHUNK = 64
CONV_KERNEL = 31
HEAD_DIM = 128
V_HEAD_DIM = 2 * HEAD_DIM
ROPE_THETA = 10000.0
EPS = 1e-6
LOG2E = math.log2(math.e)

V7X_VMEM_LIMIT_BYTES = 56 * 1024 * 1024
CONV_HALO_ROWS = 32
F32_SUBLANES = 8
ATTN_KV_UNROLL = 4
PROJ_ROW_PARTS = 2


def _tiles(seq):
    return dict(
        rope=min(1024, seq),
        proj=min(512, seq),
        conv=min(512, seq),
        attn=min(512, seq),
        merge=min(512, seq),
        mlp=min(512, seq),
        mlp_ff=1024,
    )


def _params(n_axes):
    return pltpu.CompilerParams(
        dimension_semantics=("arbitrary",) * n_axes,
        vmem_limit_bytes=V7X_VMEM_LIMIT_BYTES,
    )


def _sigmoid(v):
    return 1.0 / (1.0 + jnp.exp(-v))


def _lane_tile(a, reps):
    return jnp.concatenate([a] * reps, axis=-1) if reps > 1 else a


def _rms(v, axis_size):
    return lax.rsqrt(jnp.sum(v * v, axis=-1, keepdims=True) * (1.0 / axis_size) + EPS)


def _mod_norm(x, g, ada, shift_row, scale_row):
    y = x * _rms(x, x.shape[-1]) * g
    return y * (1.0 + ada[scale_row:scale_row + 1, :]) + ada[shift_row:shift_row + 1, :]


def _ada_kernel(c_ref, w_ref, b_ref, o_ref):
    c = c_ref[...]
    c_act = c * _sigmoid(c)
    o_ref[...] = jnp.dot(c_act.astype(BF16), w_ref[...].astype(BF16),
                         preferred_element_type=F32) + b_ref[...]


def _ada(c, ada_w, ada_b):
    b, d = c.shape
    n_out = ada_w.shape[1]
    rows = 8
    c_pad = jnp.zeros((rows, d), F32).at[:b].set(c)
    tn = 1536 if n_out % 1536 == 0 else n_out
    out = pl.pallas_call(
        _ada_kernel,
        out_shape=jax.ShapeDtypeStruct((rows, n_out), F32),
        grid=(n_out // tn,),
        in_specs=[
            pl.BlockSpec((rows, d), lambda j: (0, 0)),
            pl.BlockSpec((d, tn), lambda j: (0, j)),
            pl.BlockSpec((1, tn), lambda j: (0, j)),
        ],
        out_specs=pl.BlockSpec((rows, tn), lambda j: (0, j)),
        compiler_params=_params(1),
        name="ada",
    )(c_pad, ada_w, ada_b.reshape(1, n_out))
    return out[:b].reshape(b, 6, d)


def _rope_kernel(pos_ref, invf_ref, sign_ref, cos_ref, sin_ref):
    ang = pos_ref[...].astype(F32) * invf_ref[...]
    cos_ref[...] = jnp.cos(ang)
    sin_ref[...] = jnp.sin(ang) * sign_ref[...]


def _rope_tables(pos_flat, tm):
    n = pos_flat.shape[0]
    half = HEAD_DIM // 2
    inv_freq = ROPE_THETA ** (-jnp.arange(0, HEAD_DIM, 2, dtype=F32) / HEAD_DIM)
    invf = jnp.concatenate([inv_freq, inv_freq]).reshape(1, HEAD_DIM)
    sign = jnp.concatenate([-jnp.ones((half,), F32), jnp.ones((half,), F32)]).reshape(1, HEAD_DIM)
    pos_b = jnp.broadcast_to(pos_flat[:, None], (n, HEAD_DIM))
    return pl.pallas_call(
        _rope_kernel,
        out_shape=(jax.ShapeDtypeStruct((n, HEAD_DIM), F32),) * 2,
        grid=(n // tm,),
        in_specs=[
            pl.BlockSpec((tm, HEAD_DIM), lambda i: (i, 0)),
            pl.BlockSpec((1, HEAD_DIM), lambda i: (0, 0)),
            pl.BlockSpec((1, HEAD_DIM), lambda i: (0, 0)),
        ],
        out_specs=(pl.BlockSpec((tm, HEAD_DIM), lambda i: (i, 0)),) * 2,
        compiler_params=_params(1),
        name="rope",
    )(pos_b, invf, sign)


def _row_parts(rows):
    part = rows // PROJ_ROW_PARTS if rows % (PROJ_ROW_PARTS * 16) == 0 else rows
    return [slice(r0, r0 + part) for r0 in range(0, rows, part)]


def _conv_in_kernel(x_ref, ada_ref, g_ref, w_ref, o_ref):
    for rows in _row_parts(x_ref.shape[0]):
        h = _mod_norm(x_ref[rows, :], g_ref[...], ada_ref[0], 0, 1).astype(BF16)
        u = jnp.dot(h, w_ref[...], preferred_element_type=F32)
        half = u.shape[1] // 2
        o_ref[rows, :] = (u[:, :half] * _sigmoid(u[:, half:])).astype(BF16)


def _qkv_kernel(x_ref, ada_ref, g_ref, wq_ref, wk_ref, wv_ref, qg_ref, kg_ref,
                cos_ref, sin_ref, q_ref, k_ref, v_ref, *, q_scale):
    n_heads = q_ref.shape[1] // V_HEAD_DIM
    for rows in _row_parts(x_ref.shape[0]):
        h = _mod_norm(x_ref[rows, :], g_ref[...], ada_ref[0], 0, 1).astype(BF16)
        cos = cos_ref[rows, :]
        sin = sin_ref[rows, :]

        def norm_rope(t, g, mul):
            y = t * _rms(t, HEAD_DIM) * g
            rot = pltpu.roll(y, HEAD_DIM // 2, 1)
            return (y * cos + rot * sin) * mul

        for w_ref_, g_ref_, o_ref_, mul in ((wq_ref, qg_ref, q_ref, q_scale),
                                            (wk_ref, kg_ref, k_ref, 1.0)):
            g = g_ref_[...]
            for hd in range(n_heads):
                c0 = hd * V_HEAD_DIM
                u = jnp.dot(h, w_ref_[:, c0:c0 + V_HEAD_DIM], preferred_element_type=F32)
                for m in range(2):
                    t = u[:, m * HEAD_DIM:(m + 1) * HEAD_DIM]
                    o_ref_[rows, c0 + m * HEAD_DIM:c0 + (m + 1) * HEAD_DIM] = (
                        norm_rope(t, g, mul).astype(BF16))
        v_ref[rows, :] = jnp.dot(h, wv_ref[...], preferred_element_type=F32).astype(BF16)


def _gates_kernel(x_ref, ada_ref, g_ref, w_ref, b_ref, o_ref, *, col_chunk):
    for rows in _row_parts(x_ref.shape[0]):
        h = _mod_norm(x_ref[rows, :], g_ref[...], ada_ref[0], 0, 1).astype(BF16)
        for c0 in range(0, o_ref.shape[1], col_chunk):
            u = jnp.dot(h, w_ref[:, c0:c0 + col_chunk], preferred_element_type=F32)
            o_ref[rows, c0:c0 + col_chunk] = _sigmoid(u + b_ref[:, c0:c0 + col_chunk]).astype(BF16)


def _row_spec(tm, width):
    return pl.BlockSpec((tm, width), lambda i: (i, 0))


def _full_spec(shape):
    return pl.BlockSpec(shape, lambda i: (0,) * len(shape))


def _ada_spec(d, tiles_per_batch):
    return pl.BlockSpec((1, 6, d), lambda i: (i // tiles_per_batch, 0, 0))


def _conv_kernel(cur_ref, halo_ref, w_ref, b_ref, g_ref, o_ref, win_ref, *, rows_per_chunk):
    ts = cur_ref.shape[1]
    width = cur_ref.shape[2]
    halo = halo_ref[0].astype(F32)
    win_ref[0, 0:CONV_HALO_ROWS, :] = jnp.where(pl.program_id(1) == 0, 0.0, halo)
    win_ref[0, CONV_HALO_ROWS:CONV_HALO_ROWS + ts, :] = cur_ref[0].astype(F32)
    shifted_rows = ts + CONV_HALO_ROWS - F32_SUBLANES
    for o in range(1, F32_SUBLANES):
        for r0 in range(0, shifted_rows, rows_per_chunk):
            rows = min(rows_per_chunk, shifted_rows - r0)
            win_ref[o, r0:r0 + rows, :] = win_ref[0, r0 + o:r0 + o + rows, :]
    first = CONV_HALO_ROWS - (CONV_KERNEL - 1)
    bias = b_ref[...]
    g = g_ref[...]

    def chunk(ci, carry):
        r0 = pl.multiple_of(ci * rows_per_chunk, rows_per_chunk)
        acc = None
        for k in range(CONV_KERNEL):
            o = (first + k) % F32_SUBLANES
            base = (first + k) - o
            tap = jnp.concatenate([w_ref[k]] * (rows_per_chunk // F32_SUBLANES), axis=0)
            term = win_ref[o, pl.ds(r0 + base, rows_per_chunk), :] * tap
            acc = term if acc is None else acc + term
        y = acc + bias
        z = y * _rms(y, width) * g
        o_ref[0, pl.ds(r0, rows_per_chunk), :] = (z * _sigmoid(z)).astype(BF16)
        return carry

    lax.fori_loop(0, ts // rows_per_chunk, chunk, 0, unroll=2)


def _conv_branch(glu, conv_w, conv_b, conv_norm_g, ts):
    b, s, width = glu.shape
    halo_per_tile = ts // CONV_HALO_ROWS
    return pl.pallas_call(
        functools.partial(_conv_kernel, rows_per_chunk=32),
        out_shape=jax.ShapeDtypeStruct((b, s, width), BF16),
        grid=(b, s // ts),
        in_specs=[
            pl.BlockSpec((1, ts, width), lambda bi, i: (bi, i, 0)),
            pl.BlockSpec((1, CONV_HALO_ROWS, width),
                         lambda bi, i: (bi, jnp.maximum(i * halo_per_tile - 1, 0), 0)),
            pl.BlockSpec((CONV_KERNEL, F32_SUBLANES, width), lambda bi, i: (0, 0, 0)),
            pl.BlockSpec((1, width), lambda bi, i: (0, 0)),
            pl.BlockSpec((1, width), lambda bi, i: (0, 0)),
        ],
        out_specs=pl.BlockSpec((1, ts, width), lambda bi, i: (bi, i, 0)),
        scratch_shapes=[pltpu.VMEM((F32_SUBLANES, CONV_HALO_ROWS + ts, width), F32)],
        compiler_params=_params(2),
        name="conv",
    )(glu, glu, jnp.broadcast_to(conv_w[:, None, :], (CONV_KERNEL, F32_SUBLANES, width)),
      conv_b.reshape(1, width), conv_norm_g.reshape(1, width))


def _attn_kernel(q_ref, k_ref, v_ref, lq1_ref, lk1_ref, lq2_ref, lk2_ref, sg_ref, o_ref,
                 m_ref, l_ref, acc_ref, s_ref, *, lam_init, unroll):
    tq = q_ref.shape[0]
    tk = tq
    qi = pl.program_id(2)

    m_ref[...] = jnp.full(m_ref.shape, -jnp.inf, F32)
    l_ref[...] = jnp.zeros(l_ref.shape, F32)
    acc_ref[...] = jnp.zeros(acc_ref.shape, F32)

    def kv_tiles(tiles, last_mask=None):
        for u, j in enumerate(tiles):
            row0 = pl.multiple_of(j * tk, tk)
            for m in range(2):
                lanes = slice(m * HEAD_DIM, (m + 1) * HEAD_DIM)
                s_ref[u, m] = lax.dot_general(
                    q_ref[:, lanes], k_ref[pl.ds(row0, tk), lanes], (((1,), (1,)), ((), ())),
                    preferred_element_type=F32)
        for u, j in enumerate(tiles):
            row0 = pl.multiple_of(j * tk, tk)
            v = v_ref[pl.ds(row0, tk), :]
            for m in range(2):
                s = s_ref[u, m]
                if last_mask is not None and u == len(tiles) - 1:
                    s = jnp.where(last_mask, s, -jnp.inf)
                m_old = m_ref[m]
                m_new = jnp.maximum(m_old, jnp.max(s, axis=-1, keepdims=True))
                alpha = jnp.exp2(m_old - m_new)
                p = jnp.exp2(s - _lane_tile(m_new, tk // HEAD_DIM))
                l_ref[m] = alpha * l_ref[m] + jnp.sum(p, axis=-1, keepdims=True)
                acc_ref[m] = (_lane_tile(alpha, V_HEAD_DIM // HEAD_DIM) * acc_ref[m]
                              + jnp.dot(p.astype(BF16), v, preferred_element_type=F32))
                m_ref[m] = m_new

    def unrolled_body(jj, carry):
        kv_tiles([jj * unroll + u for u in range(unroll)])
        return carry

    n_unrolled = qi // unroll
    lax.fori_loop(0, n_unrolled, unrolled_body, 0)
    q_chunk = lax.broadcasted_iota(jnp.int32, (tq, tk), 0) // CHUNK
    k_chunk = lax.broadcasted_iota(jnp.int32, (tq, tk), 1) // CHUNK
    diag_mask = k_chunk <= q_chunk
    first_rest = n_unrolled * unroll
    for rest in range(unroll):
        @pl.when(qi - first_rest == rest)
        def _(rest=rest):
            kv_tiles([first_rest + u for u in range(rest)] + [qi], diag_mask)

    lam = (jnp.exp(jnp.sum(lq1_ref[...] * lk1_ref[...], axis=-1, keepdims=True))
           - jnp.exp(jnp.sum(lq2_ref[...] * lk2_ref[...], axis=-1, keepdims=True))
           + lam_init)
    lanes_rep = V_HEAD_DIM // HEAD_DIM
    o = (acc_ref[0] * _lane_tile(1.0 / l_ref[0], lanes_rep)
         - lam * (acc_ref[1] * _lane_tile(1.0 / l_ref[1], lanes_rep)))
    o = o * _rms(o, V_HEAD_DIM) * sg_ref[...] * (1.0 - lam_init)
    o_ref[...] = o.astype(BF16)


def _attention(q, k, v, lq1, lk1, lq2, lk2, subln_g, batch, seq, tq, lam_init):
    n, width = q.shape
    n_heads = width // V_HEAD_DIM
    nq = seq // tq
    vec = lambda a: a.reshape(1, HEAD_DIM)
    vec_spec = pl.BlockSpec((1, HEAD_DIM), lambda b, h, i: (0, 0))
    return pl.pallas_call(
        functools.partial(_attn_kernel, lam_init=lam_init, unroll=ATTN_KV_UNROLL),
        out_shape=jax.ShapeDtypeStruct((n, width), BF16),
        grid=(batch, n_heads, nq),
        in_specs=[
            pl.BlockSpec((tq, V_HEAD_DIM), lambda b, h, i: (b * nq + i, h)),
            pl.BlockSpec((seq, V_HEAD_DIM), lambda b, h, i: (b, h)),
            pl.BlockSpec((seq, V_HEAD_DIM), lambda b, h, i: (b, h)),
            vec_spec, vec_spec, vec_spec, vec_spec,
            pl.BlockSpec((1, V_HEAD_DIM), lambda b, h, i: (0, 0)),
        ],
        out_specs=pl.BlockSpec((tq, V_HEAD_DIM), lambda b, h, i: (b * nq + i, h)),
        scratch_shapes=[
            pltpu.VMEM((2, tq, HEAD_DIM), F32),
            pltpu.VMEM((2, tq, HEAD_DIM), F32),
            pltpu.VMEM((2, tq, V_HEAD_DIM), F32),
            pltpu.VMEM((ATTN_KV_UNROLL, 2, tq, tq), F32),
        ],
        compiler_params=_params(3),
        name="attn",
    )(q, k, v, vec(lq1), vec(lk1), vec(lq2), vec(lk2), subln_g.reshape(1, V_HEAD_DIM))


def _merge_kernel(x_ref, ada_ref, conv_ref, attn_ref, gates_ref, wc_ref, wa_ref, wo_ref, g_ref,
                  o_ref, h_ref):
    d = x_ref.shape[1]
    ada = ada_ref[0]
    for rows in _row_parts(x_ref.shape[0]):
        y_conv = jnp.dot(conv_ref[rows, :], wc_ref[...], preferred_element_type=F32)
        y_attn = jnp.dot(attn_ref[rows, :], wa_ref[...], preferred_element_type=F32)
        merged = (gates_ref[rows, :d].astype(F32) * y_conv
                  + gates_ref[rows, d:].astype(F32) * y_attn)
        out = jnp.dot(merged.astype(BF16), wo_ref[...], preferred_element_type=F32)
        x_mid = x_ref[rows, :] + ada[2:3, :] * out
        o_ref[rows, :] = x_mid
        h_ref[rows, :] = _mod_norm(x_mid, g_ref[...], ada, 3, 4).astype(BF16)


def _mlp_kernel(x_ref, h_ref, ada_ref, w1_ref, w2_ref, o_ref, acc_ref):
    f = pl.program_id(1)

    @pl.when(f == 0)
    def _():
        acc_ref[...] = jnp.zeros(acc_ref.shape, F32)

    a = jnp.dot(h_ref[...], w1_ref[...], preferred_element_type=F32)
    a = jnp.square(jnp.maximum(a, 0.0)).astype(BF16)
    acc_ref[...] += jnp.dot(a, w2_ref[...], preferred_element_type=F32)

    @pl.when(f == pl.num_programs(1) - 1)
    def _():
        o_ref[...] = x_ref[...] + ada_ref[0][5:6, :] * acc_ref[...]


def _layer(x2, ada, cos, sin, batch, seq, lam_init, norm_mix_g, w_in, conv_w, conv_b, conv_norm_g,
           w_conv_out, q_norm_g, k_norm_g, lq1, lk1, lq2, lk2, subln_g, w_attn_out, gate_b, w_out,
           norm_mlp_g, w_mlp_in, w_mlp_out):
    n, d = x2.shape
    t = _tiles(seq)
    conv_width = conv_w.shape[1]
    qk_width = w_attn_out.shape[0]
    d_ff = w_mlp_in.shape[1]
    c_qk = 2 * conv_width
    w_glu = w_in[:, :c_qk].astype(BF16)
    w_q = w_in[:, c_qk:c_qk + qk_width].astype(BF16)
    w_k = w_in[:, c_qk + qk_width:c_qk + 2 * qk_width].astype(BF16)
    w_v = w_in[:, c_qk + 2 * qk_width:c_qk + 3 * qk_width].astype(BF16)
    w_gate = w_in[:, c_qk + 3 * qk_width:].astype(BF16)
    g_mix = norm_mix_g.reshape(1, d)

    tm = t["proj"]
    tpb = seq // tm
    glu = pl.pallas_call(
        _conv_in_kernel,
        out_shape=jax.ShapeDtypeStruct((n, conv_width), BF16),
        grid=(n // tm,),
        in_specs=[_row_spec(tm, d), _ada_spec(d, tpb), _full_spec((1, d)),
                  _full_spec((d, c_qk))],
        out_specs=_row_spec(tm, conv_width),
        compiler_params=_params(1),
        name="conv_in",
    )(x2, ada, g_mix, w_glu)

    q_scale = LOG2E / math.sqrt(HEAD_DIM)
    q, k, v = pl.pallas_call(
        functools.partial(_qkv_kernel, q_scale=q_scale),
        out_shape=(jax.ShapeDtypeStruct((n, qk_width), BF16),) * 3,
        grid=(n // tm,),
        in_specs=[_row_spec(tm, d), _ada_spec(d, tpb), _full_spec((1, d)),
                  _full_spec((d, qk_width)), _full_spec((d, qk_width)), _full_spec((d, qk_width)),
                  _full_spec((1, HEAD_DIM)), _full_spec((1, HEAD_DIM)),
                  _row_spec(tm, HEAD_DIM), _row_spec(tm, HEAD_DIM)],
        out_specs=(_row_spec(tm, qk_width),) * 3,
        compiler_params=_params(1),
        name="qkv",
    )(x2, ada, g_mix, w_q, w_k, w_v, q_norm_g.reshape(1, HEAD_DIM), k_norm_g.reshape(1, HEAD_DIM),
      cos, sin)

    gates = pl.pallas_call(
        functools.partial(_gates_kernel, col_chunk=min(1024, 2 * d)),
        out_shape=jax.ShapeDtypeStruct((n, 2 * d), BF16),
        grid=(n // tm,),
        in_specs=[_row_spec(tm, d), _ada_spec(d, tpb), _full_spec((1, d)),
                  _full_spec((d, 2 * d)), _full_spec((1, 2 * d))],
        out_specs=_row_spec(tm, 2 * d),
        compiler_params=_params(1),
        name="gates",
    )(x2, ada, g_mix, w_gate, gate_b.reshape(1, 2 * d))

    conv_act = _conv_branch(glu.reshape(batch, seq, conv_width), conv_w, conv_b, conv_norm_g,
                            t["conv"]).reshape(n, conv_width)
    attn = _attention(q, k, v, lq1, lk1, lq2, lk2, subln_g, batch, seq, t["attn"], lam_init)

    tm = t["merge"]
    x_mid, h_mlp = pl.pallas_call(
        _merge_kernel,
        out_shape=(jax.ShapeDtypeStruct((n, d), F32), jax.ShapeDtypeStruct((n, d), BF16)),
        grid=(n // tm,),
        in_specs=[_row_spec(tm, d), _ada_spec(d, seq // tm), _row_spec(tm, conv_width),
                  _row_spec(tm, qk_width), _row_spec(tm, 2 * d),
                  _full_spec((conv_width, d)), _full_spec((qk_width, d)), _full_spec((d, d)),
                  _full_spec((1, d))],
        out_specs=(_row_spec(tm, d), _row_spec(tm, d)),
        compiler_params=_params(1),
        name="merge",
    )(x2, ada, conv_act, attn, gates, w_conv_out.astype(BF16), w_attn_out.astype(BF16),
      w_out.astype(BF16), norm_mlp_g.reshape(1, d))

    tm = t["mlp"]
    tf = min(t["mlp_ff"], d_ff)
    tpb = seq // tm
    return pl.pallas_call(
        _mlp_kernel,
        out_shape=jax.ShapeDtypeStruct((n, d), F32),
        grid=(n // tm, d_ff // tf),
        in_specs=[pl.BlockSpec((tm, d), lambda i, f: (i, 0)),
                  pl.BlockSpec((tm, d), lambda i, f: (i, 0)),
                  pl.BlockSpec((1, 6, d), lambda i, f: (i // tpb, 0, 0)),
                  pl.BlockSpec((d, tf), lambda i, f: (0, f)),
                  pl.BlockSpec((tf, d), lambda i, f: (f, 0))],
        out_specs=pl.BlockSpec((tm, d), lambda i, f: (i, 0)),
        scratch_shapes=[pltpu.VMEM((tm, d), F32)],
        compiler_params=_params(2),
        name="mlp",
    )(x_mid, h_mlp, ada, w_mlp_in.astype(BF16), w_mlp_out.astype(BF16))


def kernel(x, c, pos, ada_w, ada_b, norm_mix_g, w_in, conv_w, conv_b, conv_norm_g, w_conv_out,
           q_norm_g, k_norm_g, lambda_q1, lambda_k1, lambda_q2, lambda_k2, subln_g, w_attn_out,
           gate_b, w_out, norm_mlp_g, w_mlp_in, w_mlp_out):
    batch, seq, d = x.shape
    depth = ada_w.shape[0]
    t = _tiles(seq)
    cos, sin = _rope_tables(pos.reshape(batch * seq), t["rope"])
    x2 = x.reshape(batch * seq, d)
    for l in range(depth):
        lam_init = 0.8 - 0.6 * math.exp(-0.3 * l)
        ada = _ada(c, ada_w[l], ada_b[l])
        x2 = _layer(x2, ada, cos, sin, batch, seq, lam_init, norm_mix_g[l], w_in[l], conv_w[l],
                    conv_b[l], conv_norm_g[l], w_conv_out[l], q_norm_g[l], k_norm_g[l],
                    lambda_q1[l], lambda_k1[l], lambda_q2[l], lambda_k2[l], subln_g[l],
                    w_attn_out[l], gate_b[l], w_out[l], norm_mlp_g[l], w_mlp_in[l], w_mlp_out[l])
    return x2.reshape(batch, seq, d)
```
